```python
import math
import jax, jax.numpy as jnp
from jax import lax
import numpy as np

D_MODEL = 1024
BATCH = 4
SEQ = 4096
DEPTH = 2
DEC_BATCH = 128
DEC_SEQ = 8
PAST_LEN = 8192
PAGE_SIZE = 128

POOL_WIDTH = D_MODEL // 4
POOL_GROUPS = 4
POOL_GROUP_DIM = POOL_WIDTH // POOL_GROUPS
POOL_WINDOWS = (2, 4, 8, 16)
POOL_BUF = max(POOL_WINDOWS) - 1
HEAD_DIM = 64
N_Q_HEADS = (D_MODEL - POOL_WIDTH) // HEAD_DIM
N_KV_HEADS = 4
GQA_GROUP = N_Q_HEADS // N_KV_HEADS
ATTN_WIDTH = N_Q_HEADS * HEAD_DIM
KV_WIDTH = N_KV_HEADS * HEAD_DIM
MIX_WIDTH = POOL_WIDTH + ATTN_WIDTH
IN_WIDTH = POOL_WIDTH + ATTN_WIDTH + 2 * KV_WIDTH
WINDOW = 128
ATTN_BLOCK = 128
D_FF = 2816
N_EXPERTS = 8
TOP_K = 2
D_FF_EXPERT = 2816
N_DENSE = (DEPTH + 1) // 2
N_MOE = DEPTH // 2
EPS = 1e-6

kernel_name = "hymba_pool_swa_alibi_moe_step"


def alibi_slopes(n):
    def pow2_slopes(m):
        start = 2.0 ** (-8.0 / m)
        return [start ** (i + 1) for i in range(m)]
    if float(math.log2(n)).is_integer():
        s = pow2_slopes(n)
    else:
        c = 2 ** int(math.floor(math.log2(n)))
        s = pow2_slopes(c) + pow2_slopes(2 * c)[0::2][: n - c]
    return jnp.asarray(np.array(s, dtype=np.float32))


def rms_norm(x, g):
    xf = x.astype(jnp.float32)
    y = xf * lax.rsqrt(jnp.mean(xf * xf, axis=-1, keepdims=True) + EPS) * g.astype(jnp.float32)
    return y.astype(x.dtype)


def project(h, w_in, q_gain, k_gain):
    proj = h @ w_in
    lead = proj.shape[:-1]
    u = proj[..., :POOL_WIDTH]
    o = POOL_WIDTH
    q = proj[..., o:o + ATTN_WIDTH].reshape(lead + (N_Q_HEADS, HEAD_DIM))
    o += ATTN_WIDTH
    k = proj[..., o:o + KV_WIDTH].reshape(lead + (N_KV_HEADS, HEAD_DIM))
    o += KV_WIDTH
    v = proj[..., o:o + KV_WIDTH].reshape(lead + (N_KV_HEADS, HEAD_DIM))
    return u, rms_norm(q, q_gain), rms_norm(k, k_gain), v


def pool_mix(u_ext, start_pos, pool_w, pool_scale):
    N = u_ext.shape[0]
    T = u_ext.shape[1] - POOL_BUF
    uf = u_ext.astype(jnp.float32)
    cs = jnp.pad(jnp.cumsum(uf, axis=1), ((0, 0), (1, 0), (0, 0)))
    u_tok = uf[:, POOL_BUF:]
    pos = start_pos + jnp.arange(T)
    outs = []
    for g, w in enumerate(POOL_WINDOWS):
        sl = slice(g * POOL_GROUP_DIM, (g + 1) * POOL_GROUP_DIM)
        win = cs[:, POOL_BUF + 1:POOL_BUF + T + 1, sl] - cs[:, POOL_BUF + 1 - w:POOL_BUF + T + 1 - w, sl]
        cnt = jnp.minimum(w, pos + 1).astype(jnp.float32)[None, :, None]
        outs.append(win / cnt - u_tok[..., sl])
    pooled = jnp.stack(outs, axis=2).astype(u_ext.dtype)
    mixed = jnp.einsum('ntgc,gcd->ntgd', pooled, pool_w).reshape(N, T, POOL_WIDTH)
    return mixed * pool_scale


def attend(q, k, v, dist, valid, slopes, sinks):
    qg = q.reshape(q.shape[:-2] + (N_KV_HEADS, GQA_GROUP, HEAD_DIM))
    s = jnp.einsum('...qhgd,...khd->...hgqk', qg, k, preferred_element_type=jnp.float32) * (HEAD_DIM ** -0.5)
    sl = slopes.reshape(N_KV_HEADS, GQA_GROUP)[:, :, None, None]
    s = s - sl * dist[..., None, None, :, :]
    s = jnp.where(valid[..., None, None, :, :], s, -jnp.inf)
    sink = sinks.astype(jnp.float32).reshape(N_KV_HEADS, GQA_GROUP)[:, :, None]
    m = jnp.maximum(jnp.max(s, axis=-1), sink)
    p = jnp.exp(s - m[..., None])
    denom = jnp.sum(p, axis=-1) + jnp.exp(sink - m)
    pr = (p / denom[..., None]).astype(v.dtype)
    o = jnp.einsum('...hgqk,...khd->...qhgd', pr, v)
    return o.reshape(o.shape[:-3] + (ATTN_WIDTH,))


def attn_prompt(q, k, v, slopes, sinks):
    B, S = q.shape[:2]
    NB = S // ATTN_BLOCK
    qb = q.reshape(B, NB, ATTN_BLOCK, N_Q_HEADS, HEAD_DIM)

    def band(x):
        xp = jnp.pad(x, ((0, 0), (ATTN_BLOCK, 0), (0, 0), (0, 0)))
        prev = xp[:, :S].reshape(B, NB, ATTN_BLOCK, N_KV_HEADS, HEAD_DIM)
        cur = x.reshape(B, NB, ATTN_BLOCK, N_KV_HEADS, HEAD_DIM)
        return jnp.concatenate([prev, cur], axis=2)

    kb, vb = band(k), band(v)
    start = jnp.arange(NB)[:, None] * ATTN_BLOCK
    qpos = start + jnp.arange(ATTN_BLOCK)[None, :]
    kpos = start - ATTN_BLOCK + jnp.arange(2 * ATTN_BLOCK)[None, :]
    dist = qpos[:, :, None] - kpos[:, None, :]
    valid = (dist >= 0) & (dist < WINDOW) & (kpos[:, None, :] >= 0)
    o = attend(qb, kb, vb, dist.astype(jnp.float32), valid, slopes, sinks)
    return o.reshape(B, S, ATTN_WIDTH)


def attn_sample(q, k_ext, v_ext, slopes, sinks, n_buf):
    T = q.shape[1]
    qpos = PAST_LEN + jnp.arange(T)
    kpos = PAST_LEN - n_buf + jnp.arange(n_buf + T)
    dist = qpos[:, None] - kpos[None, :]
    valid = (dist >= 0) & (dist < WINDOW)
    return attend(q, k_ext, v_ext, dist.astype(jnp.float32), valid, slopes, sinks)


def prompt_mixers(h, w_in, q_gain, k_gain, slopes, sinks, pool_w, pool_scale):
    u, q, k, v = project(h, w_in, q_gain, k_gain)
    u_ext = jnp.pad(u, ((0, 0), (POOL_BUF, 0), (0, 0)))
    a_pool = pool_mix(u_ext, 0, pool_w, pool_scale)
    a_attn = attn_prompt(q, k, v, slopes, sinks)
    n_keep = min(WINDOW, h.shape[1])
    mix = jnp.concatenate([a_pool, a_attn], axis=-1)
    return mix, u_ext[:, -POOL_BUF:], k[:, -n_keep:], v[:, -n_keep:]


def sample_mixers(h, pool_buf, k_buf, v_buf, w_in, q_gain, k_gain, slopes, sinks, pool_w, pool_scale):
    u, q, k, v = project(h, w_in, q_gain, k_gain)
    u_ext = jnp.concatenate([pool_buf.astype(u.dtype), u], axis=1)
    a_pool = pool_mix(u_ext, PAST_LEN, pool_w, pool_scale)
    n_buf = k_buf.shape[1]
    k_ext = jnp.concatenate([k_buf.astype(k.dtype), k], axis=1)
    v_ext = jnp.concatenate([v_buf.astype(v.dtype), v], axis=1)
    a_attn = attn_sample(q, k_ext, v_ext, slopes, sinks, n_buf)
    mix = jnp.concatenate([a_pool, a_attn], axis=-1)
    return mix, u_ext[:, -POOL_BUF:], k_ext[:, -n_buf:], v_ext[:, -n_buf:]


def swiglu(h, wg, wu, wd):
    return (jax.nn.silu(h @ wg) * (h @ wu)) @ wd


def moe_swiglu(h, router, wg, wu, wd):
    logits = (h @ router).astype(jnp.float32)
    top_v, top_i = lax.top_k(logits, TOP_K)
    gates = jax.nn.softmax(top_v, axis=-1)
    combine = jnp.sum(jax.nn.one_hot(top_i, N_EXPERTS, dtype=jnp.float32) * gates[..., None], axis=-2)
    combine = combine.astype(h.dtype)
    out = jnp.zeros(h.shape, h.dtype)
    for e in range(N_EXPERTS):
        out = out + combine[..., e:e + 1] * swiglu(h, wg[e], wu[e], wd[e])
    return out


def channel_mixer(h, l, ffn_w_gate, ffn_w_up, ffn_w_down, moe_router, moe_w_gate, moe_w_up, moe_w_down):
    i = l // 2
    if l % 2 == 0:
        return swiglu(h, ffn_w_gate[i], ffn_w_up[i], ffn_w_down[i])
    return moe_swiglu(h, moe_router[i], moe_w_gate[i], moe_w_up[i], moe_w_down[i])


def setup_inputs(seed: int = 0) -> dict:
    key = jax.random.key(seed)
    ks = jax.random.split(key, 20)
    f32 = jnp.float32
    n_buf = min(WINDOW, PAST_LEN)

    def nrm(k, shape, scale):
        return jax.random.normal(k, shape, f32) * scale

    return {
        "x_prompt": nrm(ks[0], (BATCH, SEQ, D_MODEL), 1.0),
        "x_sample": nrm(ks[1], (DEC_BATCH, DEC_SEQ, D_MODEL), 1.0),
        "state_pool": nrm(ks[2], (DEPTH, DEC_BATCH, POOL_BUF, POOL_WIDTH), 1.0),
        "state_win_k": nrm(ks[3], (DEPTH, DEC_BATCH, n_buf, N_KV_HEADS, HEAD_DIM), 1.0),
        "state_win_v": nrm(ks[4], (DEPTH, DEC_BATCH, n_buf, N_KV_HEADS, HEAD_DIM), 1.0),
        "norm_mix": 1.0 + nrm(ks[5], (DEPTH, D_MODEL), 0.1),
        "w_in": nrm(ks[6], (DEPTH, D_MODEL, IN_WIDTH), D_MODEL ** -0.5),
        "q_norm": 1.0 + nrm(ks[7], (DEPTH, HEAD_DIM), 0.1),
        "k_norm": 1.0 + nrm(ks[8], (DEPTH, HEAD_DIM), 0.1),
        "attn_sinks": nrm(ks[9], (DEPTH, N_Q_HEADS), 0.5),
        "pool_w": nrm(ks[10], (DEPTH, POOL_GROUPS, POOL_GROUP_DIM, POOL_GROUP_DIM), POOL_GROUP_DIM ** -0.5),
        "pool_scale": 1.0 + nrm(ks[11], (DEPTH, POOL_WIDTH), 0.1),
        "w_out": nrm(ks[12], (DEPTH, MIX_WIDTH, D_MODEL), MIX_WIDTH ** -0.5),
        "norm_ffn": 1.0 + nrm(ks[13], (DEPTH, D_MODEL), 0.1),
        "ffn_w_gate": nrm(ks[14], (N_DENSE, D_MODEL, D_FF), D_MODEL ** -0.5),
        "ffn_w_up": nrm(ks[15], (N_DENSE, D_MODEL, D_FF), D_MODEL ** -0.5),
        "ffn_w_down": nrm(ks[16], (N_DENSE, D_FF, D_MODEL), D_FF ** -0.5),
        "moe_router": nrm(ks[17], (N_MOE, D_MODEL, N_EXPERTS), D_MODEL ** -0.5),
        "moe_w_gate": nrm(ks[18], (N_MOE, N_EXPERTS, D_MODEL, D_FF_EXPERT), D_MODEL ** -0.5),
        "moe_w_up": nrm(jax.random.fold_in(ks[19], 0), (N_MOE, N_EXPERTS, D_MODEL, D_FF_EXPERT), D_MODEL ** -0.5),
        "moe_w_down": nrm(jax.random.fold_in(ks[19], 1), (N_MOE, N_EXPERTS, D_FF_EXPERT, D_MODEL), D_FF_EXPERT ** -0.5),
    }


def reference(x_prompt, x_sample, state_pool, state_win_k, state_win_v,
              norm_mix, w_in, q_norm, k_norm, attn_sinks, pool_w, pool_scale, w_out, norm_ffn,
              ffn_w_gate, ffn_w_up, ffn_w_down, moe_router, moe_w_gate, moe_w_up, moe_w_down):
    slopes = alibi_slopes(N_Q_HEADS)
    xp, xs = x_prompt, x_sample
    pool_p, k_p, v_p, pool_s, k_s, v_s = [], [], [], [], [], []
    for l in range(DEPTH):
        h = rms_norm(xp, norm_mix[l])
        mix, pu, pk, pv = prompt_mixers(h, w_in[l], q_norm[l], k_norm[l], slopes, attn_sinks[l], pool_w[l], pool_scale[l])
        xp = xp + mix @ w_out[l]
        xp = xp + channel_mixer(rms_norm(xp, norm_ffn[l]), l, ffn_w_gate, ffn_w_up, ffn_w_down,
                                moe_router, moe_w_gate, moe_w_up, moe_w_down)
        pool_p.append(pu); k_p.append(pk); v_p.append(pv)
        h = rms_norm(xs, norm_mix[l])
        mix, su, sk, sv = sample_mixers(h, state_pool[l], state_win_k[l], state_win_v[l], w_in[l], q_norm[l], k_norm[l],
                                        slopes, attn_sinks[l], pool_w[l], pool_scale[l])
        xs = xs + mix @ w_out[l]
        xs = xs + channel_mixer(rms_norm(xs, norm_ffn[l]), l, ffn_w_gate, ffn_w_up, ffn_w_down,
                                moe_router, moe_w_gate, moe_w_up, moe_w_down)
        pool_s.append(su); k_s.append(sk); v_s.append(sv)
    return (xp, xs,
            jnp.stack(pool_p), jnp.stack(k_p), jnp.stack(v_p),
            jnp.stack(pool_s), jnp.stack(k_s), jnp.stack(v_s))
```

```python
import functools
import math

import numpy as np
import jax
import jax.numpy as jnp
from jax import lax
from jax.experimental import pallas as pl
from jax.experimental.pallas import tpu as pltpu

F32 = jnp.float32
BF16 = jnp.bfloat16

D_MODEL = 1024
DEPTH = 2
POOL_WIDTH = 256
POOL_GROUP_DIM = 64
POOL_WINDOWS = (2, 4, 8, 16)
POOL_BUF = 15
HEAD_DIM = 64
N_Q_HEADS = 12
N_KV_HEADS = 4
GQA_GROUP = 3
ATTN_WIDTH = 768
KV_WIDTH = 256
QK_WIDTH = ATTN_WIDTH + KV_WIDTH
IN_WIDTH = 1536
WINDOW = 128
D_FF = 2816
N_EXPERTS = 8
EPS = 1e-6
NEG_INF = float("-inf")

LANES = 128
VMEM_LIMIT = 56 * 1024 * 1024


def _alibi_slopes(n):
    def pow2_slopes(m):
        start = 2.0 ** (-8.0 / m)
        return [start ** (i + 1) for i in range(m)]
    if float(math.log2(n)).is_integer():
        s = pow2_slopes(n)
    else:
        c = 2 ** int(math.floor(math.log2(n)))
        s = pow2_slopes(c) + pow2_slopes(2 * c)[0::2][: n - c]
    return np.array(s, dtype=np.float32)


def _split_bf16(x):
    hi = x.astype(BF16)
    lo = (x - hi.astype(F32)).astype(BF16)
    return hi, lo


def _params(*sem):
    return pltpu.CompilerParams(dimension_semantics=sem, vmem_limit_bytes=VMEM_LIMIT)


def _proj_kernel(x_ref, g_ref, w_ref, sel_ref, selt_ref, gain_ref,
                 u_ref, q_ref, k_ref, v_ref):
    xf = x_ref[...]
    ms = jnp.mean(xf * xf, axis=-1, keepdims=True)
    h = (xf * lax.rsqrt(ms + EPS) * g_ref[...]).astype(BF16)
    proj = jnp.dot(h, w_ref[...], preferred_element_type=F32)
    u_ref[...] = proj[:, :POOL_WIDTH]
    v_ref[...] = proj[:, POOL_WIDTH + QK_WIDTH:]
    qk = proj[:, POOL_WIDTH:POOL_WIDTH + QK_WIDTH]
    hi, lo = _split_bf16(qk * qk)
    ss = (jnp.dot(hi, sel_ref[...], preferred_element_type=F32)
          + jnp.dot(lo, sel_ref[...], preferred_element_type=F32))
    scale = lax.rsqrt(ss * (1.0 / HEAD_DIM) + EPS)
    shi, slo = _split_bf16(scale)
    full = (jnp.dot(shi, selt_ref[...], preferred_element_type=F32)
            + jnp.dot(slo, selt_ref[...], preferred_element_type=F32))
    qkn = qk * full * gain_ref[...]
    q_ref[...] = qkn[:, :ATTN_WIDTH].astype(q_ref.dtype)
    k_ref[...] = qkn[:, ATTN_WIDTH:]


def _project(x, g, w_in_b, sel, selt, gain, *, tm, q_dtype):
    n = x.shape[0]
    const = lambda i: (0, 0)
    row = lambda i: (i, 0)
    return pl.pallas_call(
        _proj_kernel,
        grid=(n // tm,),
        in_specs=[
            pl.BlockSpec((tm, D_MODEL), row),
            pl.BlockSpec((1, D_MODEL), const),
            pl.BlockSpec((D_MODEL, IN_WIDTH), const),
            pl.BlockSpec((QK_WIDTH, LANES), const),
            pl.BlockSpec((LANES, QK_WIDTH), const),
            pl.BlockSpec((1, QK_WIDTH), const),
        ],
        out_specs=[
            pl.BlockSpec((tm, POOL_WIDTH), row),
            pl.BlockSpec((tm, ATTN_WIDTH), row),
            pl.BlockSpec((tm, KV_WIDTH), row),
            pl.BlockSpec((tm, KV_WIDTH), row),
        ],
        out_shape=[
            jax.ShapeDtypeStruct((n, POOL_WIDTH), F32),
            jax.ShapeDtypeStruct((n, ATTN_WIDTH), q_dtype),
            jax.ShapeDtypeStruct((n, KV_WIDTH), F32),
            jax.ShapeDtypeStruct((n, KV_WIDTH), F32),
        ],
        compiler_params=_params("arbitrary"),
        name="norm_proj",
    )(x, g, w_in_b, sel, selt, gain)


def _pool_window_sums(load_shifted):
    x0 = load_shifted(0)
    acc = x0 + load_shifted(1)
    s2 = acc
    for d in range(2, 4):
        acc = acc + load_shifted(d)
    s4 = acc
    for d in range(4, 8):
        acc = acc + load_shifted(d)
    s8 = acc
    for d in range(8, 16):
        acc = acc + load_shifted(d)
    s16 = acc
    lane = lax.broadcasted_iota(jnp.int32, x0.shape, 1)
    win = jnp.where(lane < 64, s2, jnp.where(lane < 128, s4, jnp.where(lane < 192, s8, s16)))
    return x0, win, lane


def _pool_window_len(lane):
    return jnp.where(lane < 64, 2, jnp.where(lane < 128, 4, jnp.where(lane < 192, 8, 16)))


def _prompt_mix_kernel(sink_ref, x_ref, q_ref, kc_ref, kp_ref, vc_ref, vp_ref, uc_ref, up_ref,
                       bias_ref, poolw_ref, pscale_ref, wout_ref,
                       o_ref, kk, vv, uu, mix, *, tq):
    i = pl.program_id(1)
    first = i == 0
    blk = WINDOW
    kk[0:blk] = kp_ref[...].astype(BF16)
    kk[blk:] = kc_ref[...].astype(BF16)
    vv[0:blk] = vp_ref[...].astype(BF16)
    vv[blk:] = vc_ref[...].astype(BF16)
    uu[0:16] = jnp.where(first, 0.0, up_ref[...])
    uu[16:] = uc_ref[...]

    col = lax.broadcasted_iota(jnp.int32, (1, 2 * blk), 1)
    no_prev = jnp.where(jnp.logical_and(first, col < blk), NEG_INF, 0.0)

    for j in range(tq // blk):
        r0 = j * blk
        x0, win, lane = _pool_window_sums(lambda d: uu[16 + r0 - d:16 + r0 - d + blk, :])
        pos = i * tq + r0 + lax.broadcasted_iota(jnp.int32, x0.shape, 0)
        cnt = jnp.minimum(_pool_window_len(lane), pos + 1).astype(F32)
        pooled = (win / cnt - x0).astype(BF16)
        pm = jnp.dot(pooled, poolw_ref[...], preferred_element_type=F32) * pscale_ref[...]
        mix[r0:r0 + blk, 0:POOL_WIDTH] = pm.astype(BF16)
        for g in range(N_KV_HEADS):
            qs = jnp.concatenate(
                [q_ref[r0:r0 + blk, HEAD_DIM * (GQA_GROUP * g + r):HEAD_DIM * (GQA_GROUP * g + r + 1)]
                 for r in range(GQA_GROUP)], axis=0)
            kh = kk[r0:r0 + 2 * blk, HEAD_DIM * g:HEAD_DIM * (g + 1)]
            vh = vv[r0:r0 + 2 * blk, HEAD_DIM * g:HEAD_DIM * (g + 1)]
            s = lax.dot_general(qs, kh, (((1,), (1,)), ((), ())),
                                preferred_element_type=F32)
            s = s + bias_ref[g]
            if j == 0:
                s = s + no_prev
            ps, dens = [], []
            for r in range(GQA_GROUP):
                sr = s[r * blk:(r + 1) * blk]
                sink = sink_ref[GQA_GROUP * g + r]
                m = jnp.maximum(jnp.max(sr, axis=-1, keepdims=True), sink)
                p = jnp.exp(sr - m)
                dens.append(jnp.sum(p, axis=-1, keepdims=True) + jnp.exp(sink - m))
                ps.append(p.astype(BF16))
            o = jnp.dot(jnp.concatenate(ps, axis=0), vh, preferred_element_type=F32)
            for r in range(GQA_GROUP):
                c0 = POOL_WIDTH + HEAD_DIM * (GQA_GROUP * g + r)
                mix[r0:r0 + blk, c0:c0 + HEAD_DIM] = (o[r * blk:(r + 1) * blk] / dens[r]).astype(BF16)

    o_ref[...] = x_ref[...] + jnp.dot(mix[...], wout_ref[...], preferred_element_type=F32)


def _prompt_mix(x, q, k, v, u, sinks, bias, poolw_bd, pscale, wout_b, *, batch, seq, tq):
    nb = seq // tq
    x3 = x.reshape(batch, seq, D_MODEL)
    q3 = q.reshape(batch, seq, ATTN_WIDTH)
    k3 = k.reshape(batch, seq, KV_WIDTH)
    v3 = v.reshape(batch, seq, KV_WIDTH)
    u3 = u.reshape(batch, seq, POOL_WIDTH)
    cur = lambda b, i: (b, i, 0)
    prev_kv = lambda b, i: (b, jnp.maximum(i * (tq // WINDOW) - 1, 0), 0)
    prev_u = lambda b, i: (b, jnp.maximum(i * (tq // 16) - 1, 0), 0)
    const2 = lambda b, i: (0, 0)
    const3 = lambda b, i: (0, 0, 0)
    out = pl.pallas_call(
        functools.partial(_prompt_mix_kernel, tq=tq),
        grid=(batch, nb),
        in_specs=[
            pl.BlockSpec(memory_space=pltpu.SMEM),
            pl.BlockSpec((None, tq, D_MODEL), cur),
            pl.BlockSpec((None, tq, ATTN_WIDTH), cur),
            pl.BlockSpec((None, tq, KV_WIDTH), cur),
            pl.BlockSpec((None, WINDOW, KV_WIDTH), prev_kv),
            pl.BlockSpec((None, tq, KV_WIDTH), cur),
            pl.BlockSpec((None, WINDOW, KV_WIDTH), prev_kv),
            pl.BlockSpec((None, tq, POOL_WIDTH), cur),
            pl.BlockSpec((None, 16, POOL_WIDTH), prev_u),
            pl.BlockSpec((N_KV_HEADS, GQA_GROUP * WINDOW, 2 * WINDOW), const3),
            pl.BlockSpec((POOL_WIDTH, POOL_WIDTH), const2),
            pl.BlockSpec((1, POOL_WIDTH), const2),
            pl.BlockSpec((D_MODEL, D_MODEL), const2),
        ],
        out_specs=pl.BlockSpec((None, tq, D_MODEL), cur),
        out_shape=jax.ShapeDtypeStruct((batch, seq, D_MODEL), F32),
        scratch_shapes=[
            pltpu.VMEM((tq + WINDOW, KV_WIDTH), BF16),
            pltpu.VMEM((tq + WINDOW, KV_WIDTH), BF16),
            pltpu.VMEM((tq + 16, POOL_WIDTH), F32),
            pltpu.VMEM((tq, D_MODEL), BF16),
        ],
        compiler_params=_params("arbitrary", "arbitrary"),
        name="prompt_mix",
    )(sinks, x3, q3, k3, k3, v3, v3, u3, u3, bias, poolw_bd, pscale, wout_b)
    return out.reshape(batch * seq, D_MODEL)


SAMPLE_ROWS = 32
SAMPLE_KEYS = 256


def _sample_mix_kernel(x_ref, q_ref, kn_ref, vn_ref, un_ref, kb_ref, vb_ref, pb_ref,
                       bias_ref, sinkcol_ref, poolw_ref, pscale_ref, wout_ref,
                       o_ref, ko_ref, vo_ref, po_ref,
                       kx, vx, px, pooled_s, mix, *, group, t_new, n_buf):
    kx[n_buf:, :] = jnp.zeros((SAMPLE_KEYS - n_buf, KV_WIDTH), F32)
    vx[n_buf:, :] = jnp.zeros((SAMPLE_KEYS - n_buf, KV_WIDTH), F32)

    def per_seq(n, carry):
        row0 = pl.multiple_of(n * t_new, t_new)
        kx[0:n_buf, :] = kb_ref[n]
        kx[n_buf:n_buf + t_new, :] = kn_ref[n]
        vx[0:n_buf, :] = vb_ref[n]
        vx[n_buf:n_buf + t_new, :] = vn_ref[n]
        ko_ref[n] = kx[t_new:t_new + n_buf, :]
        vo_ref[n] = vx[t_new:t_new + n_buf, :]
        px[1:16, :] = pb_ref[n]
        px[16:16 + t_new, :] = un_ref[n]
        po_ref[n] = px[16 + t_new - POOL_BUF:16 + t_new, :]
        x0, win, lane = _pool_window_sums(lambda d: px[16 - d:16 - d + t_new, :])
        cnt = _pool_window_len(lane).astype(F32)
        pooled_s[pl.ds(row0, t_new), :] = win / cnt - x0
        qn = q_ref[n]
        for g in range(N_KV_HEADS):
            heads = [qn[:, HEAD_DIM * (GQA_GROUP * g + r):HEAD_DIM * (GQA_GROUP * g + r + 1)]
                     for r in range(GQA_GROUP)]
            qs = jnp.concatenate(heads + [heads[-1]], axis=0).astype(BF16)
            kh = kx[:, HEAD_DIM * g:HEAD_DIM * (g + 1)].astype(BF16)
            vh = vx[:, HEAD_DIM * g:HEAD_DIM * (g + 1)].astype(BF16)
            s = lax.dot_general(qs, kh, (((1,), (1,)), ((), ())),
                                preferred_element_type=F32) + bias_ref[g]
            sink = sinkcol_ref[g][:, 0:1]
            m = jnp.maximum(jnp.max(s, axis=-1, keepdims=True), sink)
            p = jnp.exp(s - m)
            den = jnp.sum(p, axis=-1, keepdims=True) + jnp.exp(sink - m)
            o = jnp.dot(p.astype(BF16), vh, preferred_element_type=F32) / den
            for r in range(GQA_GROUP):
                c0 = POOL_WIDTH + HEAD_DIM * (GQA_GROUP * g + r)
                mix[pl.ds(row0, t_new), c0:c0 + HEAD_DIM] = o[r * t_new:(r + 1) * t_new]
        return carry

    lax.fori_loop(0, group, per_seq, 0)
    pm = jnp.dot(pooled_s[...].astype(BF16), poolw_ref[...], preferred_element_type=F32) * pscale_ref[...]
    mix[:, 0:POOL_WIDTH] = pm
    o_ref[...] = x_ref[...] + jnp.dot(mix[...].astype(BF16), wout_ref[...], preferred_element_type=F32)


def _sample_mix(x, q, k, v, u, kbuf, vbuf, pbuf, bias, sinkcol, poolw_bd, pscale, wout_b,
                *, n_seq, t_new, group):
    n_buf = kbuf.shape[1]
    q3 = q.reshape(n_seq, t_new, ATTN_WIDTH)
    k3 = k.reshape(n_seq, t_new, KV_WIDTH)
    v3 = v.reshape(n_seq, t_new, KV_WIDTH)
    u3 = u.reshape(n_seq, t_new, POOL_WIDTH)
    rows = group * t_new
    seq3 = lambda i: (i, 0, 0)
    row2 = lambda i: (i, 0)
    const2 = lambda i: (0, 0)
    const3 = lambda i: (0, 0, 0)
    return pl.pallas_call(
        functools.partial(_sample_mix_kernel, group=group, t_new=t_new, n_buf=n_buf),
        grid=(n_seq // group,),
        in_specs=[
            pl.BlockSpec((rows, D_MODEL), row2),
            pl.BlockSpec((group, t_new, ATTN_WIDTH), seq3),
            pl.BlockSpec((group, t_new, KV_WIDTH), seq3),
            pl.BlockSpec((group, t_new, KV_WIDTH), seq3),
            pl.BlockSpec((group, t_new, POOL_WIDTH), seq3),
            pl.BlockSpec((group, n_buf, KV_WIDTH), seq3),
            pl.BlockSpec((group, n_buf, KV_WIDTH), seq3),
            pl.BlockSpec((group, POOL_BUF, POOL_WIDTH), seq3),
            pl.BlockSpec((N_KV_HEADS, SAMPLE_ROWS, SAMPLE_KEYS), const3),
            pl.BlockSpec((N_KV_HEADS, SAMPLE_ROWS, LANES), const3),
            pl.BlockSpec((POOL_WIDTH, POOL_WIDTH), const2),
            pl.BlockSpec((1, POOL_WIDTH), const2),
            pl.BlockSpec((D_MODEL, D_MODEL), const2),
        ],
        out_specs=[
            pl.BlockSpec((rows, D_MODEL), row2),
            pl.BlockSpec((group, n_buf, KV_WIDTH), seq3),
            pl.BlockSpec((group, n_buf, KV_WIDTH), seq3),
            pl.BlockSpec((group, POOL_BUF, POOL_WIDTH), seq3),
        ],
        out_shape=[
            jax.ShapeDtypeStruct((n_seq * t_new, D_MODEL), F32),
            jax.ShapeDtypeStruct((n_seq, n_buf, KV_WIDTH), F32),
            jax.ShapeDtypeStruct((n_seq, n_buf, KV_WIDTH), F32),
            jax.ShapeDtypeStruct((n_seq, POOL_BUF, POOL_WIDTH), F32),
        ],
        scratch_shapes=[
            pltpu.VMEM((SAMPLE_KEYS, KV_WIDTH), F32),
            pltpu.VMEM((SAMPLE_KEYS, KV_WIDTH), F32),
            pltpu.VMEM((16 + t_new, POOL_WIDTH), F32),
            pltpu.VMEM((rows, POOL_WIDTH), F32),
            pltpu.VMEM((rows, D_MODEL), F32),
        ],
        compiler_params=_params("arbitrary"),
        name="sample_mix",
    )(x, q3, k3, v3, u3, kbuf, vbuf, pbuf, bias, sinkcol, poolw_bd, pscale, wout_b)


def _ffn_kernel(x_ref, g_ref, rhi_ref, rlo_ref, wg_ref, wu_ref, wd_ref, o_ref,
                hb, acc, comb, *, routed):
    e = pl.program_id(1)
    f = pl.program_id(2)
    start = jnp.logical_and(e == 0, f == 0)
    last = jnp.logical_and(e == pl.num_programs(1) - 1, f == pl.num_programs(2) - 1)

    @pl.when(start)
    def _():
        xf = x_ref[...]
        ms = jnp.mean(xf * xf, axis=-1, keepdims=True)
        h = xf * lax.rsqrt(ms + EPS) * g_ref[...]
        hb[...] = h.astype(BF16)
        acc[...] = jnp.zeros_like(acc)
        if routed:
            hi, lo = _split_bf16(h)
            lg = (jnp.dot(hi, rhi_ref[...], preferred_element_type=F32)
                  + jnp.dot(lo, rhi_ref[...], preferred_element_type=F32)
                  + jnp.dot(hi, rlo_ref[...], preferred_element_type=F32))
            lane = lax.broadcasted_iota(jnp.int32, lg.shape, 1)
            lg = jnp.where(lane < N_EXPERTS, lg, NEG_INF)
            m1 = jnp.max(lg, axis=-1, keepdims=True)
            i1 = jnp.min(jnp.where(lg == m1, lane, LANES), axis=-1, keepdims=True)
            lg2 = jnp.where(lane == i1, NEG_INF, lg)
            m2 = jnp.max(lg2, axis=-1, keepdims=True)
            i2 = jnp.min(jnp.where(lg2 == m2, lane, LANES), axis=-1, keepdims=True)
            e2 = jnp.exp(m2 - m1)
            den = 1.0 + e2
            comb[...] = (jnp.where(lane == i1, 1.0 / den, 0.0)
                         + jnp.where(lane == i2, e2 / den, 0.0))

    h = hb[...]
    gate = jnp.dot(h, wg_ref[...], preferred_element_type=F32)
    up = jnp.dot(h, wu_ref[...], preferred_element_type=F32)
    act = (gate * jax.nn.sigmoid(gate) * up).astype(BF16)
    y = jnp.dot(act, wd_ref[...], preferred_element_type=F32)
    if routed:
        lane = lax.broadcasted_iota(jnp.int32, comb.shape, 1)
        c = jnp.sum(jnp.where(lane == e, comb[...], 0.0), axis=-1, keepdims=True)
        y = y * c
    acc[...] += y

    @pl.when(last)
    def _():
        o_ref[...] = x_ref[...] + acc[...]


def _ffn(x, g, rhi, rlo, wg, wu, wd, *, tm, tf, routed):
    n = x.shape[0]
    ne, _, dff = wg.shape
    row = lambda i, e, f: (i, 0)
    const = lambda i, e, f: (0, 0)
    return pl.pallas_call(
        functools.partial(_ffn_kernel, routed=routed),
        grid=(n // tm, ne, dff // tf),
        in_specs=[
            pl.BlockSpec((tm, D_MODEL), row),
            pl.BlockSpec((1, D_MODEL), const),
            pl.BlockSpec((D_MODEL, LANES), const),
            pl.BlockSpec((D_MODEL, LANES), const),
            pl.BlockSpec((None, D_MODEL, tf), lambda i, e, f: (e, 0, f)),
            pl.BlockSpec((None, D_MODEL, tf), lambda i, e, f: (e, 0, f)),
            pl.BlockSpec((None, tf, D_MODEL), lambda i, e, f: (e, f, 0)),
        ],
        out_specs=pl.BlockSpec((tm, D_MODEL), row),
        out_shape=jax.ShapeDtypeStruct((n, D_MODEL), F32),
        scratch_shapes=[
            pltpu.VMEM((tm, D_MODEL), BF16),
            pltpu.VMEM((tm, D_MODEL), F32),
            pltpu.VMEM((tm, LANES), F32),
        ],
        compiler_params=_params("arbitrary", "arbitrary", "arbitrary"),
        name="moe_ffn" if routed else "dense_ffn",
    )(x, g, rhi, rlo, wg, wu, wd)


def _head_selectors():
    sel = np.zeros((QK_WIDTH, LANES), np.float32)
    sel[np.arange(QK_WIDTH), np.arange(QK_WIDTH) // HEAD_DIM] = 1.0
    return jnp.asarray(sel, BF16), jnp.asarray(sel.T.copy(), BF16)


def _prompt_bias():
    slopes = _alibi_slopes(N_Q_HEADS)
    qi = np.arange(WINDOW)[:, None]
    kj = np.arange(2 * WINDOW)[None, :]
    dist = (qi + WINDOW - kj).astype(np.float32)
    valid = (dist >= 0) & (dist < WINDOW)
    out = np.empty((N_KV_HEADS, GQA_GROUP * WINDOW, 2 * WINDOW), np.float32)
    for g in range(N_KV_HEADS):
        for r in range(GQA_GROUP):
            b = np.where(valid, -(slopes[GQA_GROUP * g + r] * dist), -np.inf)
            out[g, r * WINDOW:(r + 1) * WINDOW] = b
    return jnp.asarray(out)


def _sample_bias(t_new, n_buf):
    slopes = _alibi_slopes(N_Q_HEADS)
    t = np.arange(t_new)[:, None]
    j = np.arange(SAMPLE_KEYS)[None, :]
    dist = (t + n_buf - j).astype(np.float32)
    valid = (dist >= 0) & (dist < WINDOW) & (j < n_buf + t_new)
    out = np.empty((N_KV_HEADS, SAMPLE_ROWS, SAMPLE_KEYS), np.float32)
    for g in range(N_KV_HEADS):
        for r in range(GQA_GROUP + 1):
            h = GQA_GROUP * g + min(r, GQA_GROUP - 1)
            out[g, r * t_new:(r + 1) * t_new] = np.where(valid, -(slopes[h] * dist), -np.inf)
    return jnp.asarray(out)


def _block_diag_pool(pool_w_l):
    bd = jnp.zeros((POOL_WIDTH, POOL_WIDTH), F32)
    for g in range(len(POOL_WINDOWS)):
        sl = slice(g * POOL_GROUP_DIM, (g + 1) * POOL_GROUP_DIM)
        bd = bd.at[sl, sl].set(pool_w_l[g])
    return bd.astype(BF16)


def kernel(x_prompt, x_sample, state_pool, state_win_k, state_win_v, norm_mix, w_in, q_norm, k_norm,
           attn_sinks, pool_w, pool_scale, w_out, norm_ffn, ffn_w_gate, ffn_w_up, ffn_w_down,
           moe_router, moe_w_gate, moe_w_up, moe_w_down):
    batch, seq, _ = x_prompt.shape
    n_seq, t_new, _ = x_sample.shape
    n_buf = state_win_k.shape[2]
    n_keep = min(WINDOW, seq)
    assert t_new == 8 and n_buf == WINDOW and SAMPLE_ROWS == (GQA_GROUP + 1) * t_new

    sel, selt = _head_selectors()
    bias_p = _prompt_bias()
    bias_s = _sample_bias(t_new, n_buf)
    zeros_r = jnp.zeros((D_MODEL, LANES), BF16)

    xp = x_prompt.reshape(batch * seq, D_MODEL)
    xs = x_sample.reshape(n_seq * t_new, D_MODEL)
    pool_p, k_p, v_p, pool_s, k_s, v_s = [], [], [], [], [], []
    for l in range(DEPTH):
        g_mix = norm_mix[l].reshape(1, D_MODEL)
        g_ffn = norm_ffn[l].reshape(1, D_MODEL)
        w_in_b = w_in[l].astype(BF16)
        w_out_b = w_out[l].astype(BF16)
        gain = jnp.concatenate([jnp.tile(q_norm[l], N_Q_HEADS) * (HEAD_DIM ** -0.5),
                                jnp.tile(k_norm[l], N_KV_HEADS)]).reshape(1, QK_WIDTH)
        poolw_bd = _block_diag_pool(pool_w[l])
        pscale = pool_scale[l].reshape(1, POOL_WIDTH)
        sinks = attn_sinks[l].astype(F32)
        sink_rows = jnp.repeat(
            jnp.concatenate([sinks.reshape(N_KV_HEADS, GQA_GROUP), sinks.reshape(N_KV_HEADS, GQA_GROUP)[:, -1:]], axis=1),
            t_new, axis=1)
        sinkcol = jnp.broadcast_to(sink_rows[:, :, None], (N_KV_HEADS, SAMPLE_ROWS, LANES))

        if l % 2 == 0:
            i = l // 2
            wg = ffn_w_gate[i][None].astype(BF16)
            wu = ffn_w_up[i][None].astype(BF16)
            wd = ffn_w_down[i][None].astype(BF16)
            rhi, rlo, routed = zeros_r, zeros_r, False
        else:
            i = l // 2
            wg = moe_w_gate[i].astype(BF16)
            wu = moe_w_up[i].astype(BF16)
            wd = moe_w_down[i].astype(BF16)
            rpad = jnp.pad(moe_router[i], ((0, 0), (0, LANES - N_EXPERTS)))
            rhi, rlo = _split_bf16(rpad)
            routed = True

        u, q, k, v = _project(xp, g_mix, w_in_b, sel, selt, gain, tm=512, q_dtype=BF16)
        xp = _prompt_mix(xp, q, k, v, u, sinks, bias_p, poolw_bd, pscale, w_out_b,
                         batch=batch, seq=seq, tq=512)
        xp = _ffn(xp, g_ffn, rhi, rlo, wg, wu, wd, tm=512, tf=1408, routed=routed)
        pool_p.append(u.reshape(batch, seq, POOL_WIDTH)[:, seq - POOL_BUF:])
        k_p.append(k.reshape(batch, seq, N_KV_HEADS, HEAD_DIM)[:, seq - n_keep:])
        v_p.append(v.reshape(batch, seq, N_KV_HEADS, HEAD_DIM)[:, seq - n_keep:])

        u, q, k, v = _project(xs, g_mix, w_in_b, sel, selt, gain, tm=512, q_dtype=F32)
        xs, ko, vo, po = _sample_mix(
            xs, q, k, v, u,
            state_win_k[l].reshape(n_seq, n_buf, KV_WIDTH),
            state_win_v[l].reshape(n_seq, n_buf, KV_WIDTH),
            state_pool[l], bias_s, sinkcol, poolw_bd, pscale, w_out_b,
            n_seq=n_seq, t_new=t_new, group=16)
        xs = _ffn(xs, g_ffn, rhi, rlo, wg, wu, wd, tm=512, tf=1408, routed=routed)
        pool_s.append(po)
        k_s.append(ko.reshape(n_seq, n_buf, N_KV_HEADS, HEAD_DIM))
        v_s.append(vo.reshape(n_seq, n_buf, N_KV_HEADS, HEAD_DIM))

    return (xp.reshape(batch, seq, D_MODEL), xs.reshape(n_seq, t_new, D_MODEL),
            jnp.stack(pool_p), jnp.stack(k_p), jnp.stack(v_p),
            jnp.stack(pool_s), jnp.stack(k_s), jnp.stack(v_s))
```

```python
import functools
import math

import numpy as np
import jax
import jax.numpy as jnp
from jax import lax
from jax.experimental import pallas as pl
from jax.experimental.pallas import tpu as pltpu

F32 = jnp.float32
BF16 = jnp.bfloat16

D_MODEL = 1024
DEPTH = 2
POOL_WIDTH = 256
POOL_GROUP_DIM = 64
POOL_WINDOWS = (2, 4, 8, 16)
POOL_BUF = 15
HEAD_DIM = 64
N_Q_HEADS = 12
N_KV_HEADS = 4
GQA_GROUP = 3
ATTN_WIDTH = 768
KV_WIDTH = 256
QK_WIDTH = ATTN_WIDTH + KV_WIDTH
IN_WIDTH = 1536
WINDOW = 128
D_FF = 2816
N_EXPERTS = 8
EPS = 1e-6
NEG_INF = float("-inf")

LANES = 128
VMEM_LIMIT = 56 * 1024 * 1024


def _alibi_slopes(n):
    def pow2_slopes(m):
        start = 2.0 ** (-8.0 / m)
        return [start ** (i + 1) for i in range(m)]
    if float(math.log2(n)).is_integer():
        s = pow2_slopes(n)
    else:
        c = 2 ** int(math.floor(math.log2(n)))
        s = pow2_slopes(c) + pow2_slopes(2 * c)[0::2][: n - c]
    return np.array(s, dtype=np.float32)


def _split_bf16(x):
    hi = x.astype(BF16)
    lo = (x - hi.astype(F32)).astype(BF16)
    return hi, lo


def _params(*sem):
    return pltpu.CompilerParams(dimension_semantics=sem, vmem_limit_bytes=VMEM_LIMIT)


def _proj_kernel(x_ref, g_ref, w_ref, sel_ref, selt_ref, gain_ref,
                 u_ref, q_ref, k_ref, v_ref):
    xf = x_ref[...]
    ms = jnp.mean(xf * xf, axis=-1, keepdims=True)
    h = (xf * lax.rsqrt(ms + EPS) * g_ref[...]).astype(BF16)
    proj = jnp.dot(h, w_ref[...], preferred_element_type=F32)
    u_ref[...] = proj[:, :POOL_WIDTH]
    v_ref[...] = proj[:, POOL_WIDTH + QK_WIDTH:]
    qk = proj[:, POOL_WIDTH:POOL_WIDTH + QK_WIDTH]
    hi, lo = _split_bf16(qk * qk)
    ss = (jnp.dot(hi, sel_ref[...], preferred_element_type=F32)
          + jnp.dot(lo, sel_ref[...], preferred_element_type=F32))
    scale = lax.rsqrt(ss * (1.0 / HEAD_DIM) + EPS)
    shi, slo = _split_bf16(scale)
    full = (jnp.dot(shi, selt_ref[...], preferred_element_type=F32)
            + jnp.dot(slo, selt_ref[...], preferred_element_type=F32))
    qkn = qk * full * gain_ref[...]
    q_ref[...] = qkn[:, :ATTN_WIDTH].astype(q_ref.dtype)
    k_ref[...] = qkn[:, ATTN_WIDTH:]


def _project(x, g, w_in_b, sel, selt, gain, *, tm, q_dtype):
    n = x.shape[0]
    const = lambda i: (0, 0)
    row = lambda i: (i, 0)
    return pl.pallas_call(
        _proj_kernel,
        grid=(n // tm,),
        in_specs=[
            pl.BlockSpec((tm, D_MODEL), row),
            pl.BlockSpec((1, D_MODEL), const),
            pl.BlockSpec((D_MODEL, IN_WIDTH), const),
            pl.BlockSpec((QK_WIDTH, LANES), const),
            pl.BlockSpec((LANES, QK_WIDTH), const),
            pl.BlockSpec((1, QK_WIDTH), const),
        ],
        out_specs=[
            pl.BlockSpec((tm, POOL_WIDTH), row),
            pl.BlockSpec((tm, ATTN_WIDTH), row),
            pl.BlockSpec((tm, KV_WIDTH), row),
            pl.BlockSpec((tm, KV_WIDTH), row),
        ],
        out_shape=[
            jax.ShapeDtypeStruct((n, POOL_WIDTH), F32),
            jax.ShapeDtypeStruct((n, ATTN_WIDTH), q_dtype),
            jax.ShapeDtypeStruct((n, KV_WIDTH), F32),
            jax.ShapeDtypeStruct((n, KV_WIDTH), F32),
        ],
        compiler_params=_params("arbitrary"),
        name="norm_proj",
    )(x, g, w_in_b, sel, selt, gain)


def _pool_window_sums(load_shifted):
    x0 = load_shifted(0)
    acc = x0 + load_shifted(1)
    s2 = acc
    for d in range(2, 4):
        acc = acc + load_shifted(d)
    s4 = acc
    for d in range(4, 8):
        acc = acc + load_shifted(d)
    s8 = acc
    for d in range(8, 16):
        acc = acc + load_shifted(d)
    s16 = acc
    lane = lax.broadcasted_iota(jnp.int32, x0.shape, 1)
    win = jnp.where(lane < 64, s2, jnp.where(lane < 128, s4, jnp.where(lane < 192, s8, s16)))
    return x0, win, lane


def _pool_window_len(lane):
    return jnp.where(lane < 64, 2, jnp.where(lane < 128, 4, jnp.where(lane < 192, 8, 16)))


def _prompt_mix_kernel(sink_ref, x_ref, q_ref, kc_ref, kp_ref, vc_ref, vp_ref, uc_ref, up_ref,
                       bias_ref, poolw_ref, pscale_ref, wout_ref,
                       o_ref, kk, vv, uu, mix, *, tq):
    i = pl.program_id(1)
    first = i == 0
    blk = WINDOW
    kk[0:blk] = kp_ref[...].astype(BF16)
    kk[blk:] = kc_ref[...].astype(BF16)
    vv[0:blk] = vp_ref[...].astype(BF16)
    vv[blk:] = vc_ref[...].astype(BF16)
    uu[0:16] = jnp.where(first, 0.0, up_ref[...])
    uu[16:] = uc_ref[...]

    col = lax.broadcasted_iota(jnp.int32, (1, 2 * blk), 1)
    no_prev = jnp.where(jnp.logical_and(first, col < blk), NEG_INF, 0.0)

    for j in range(tq // blk):
        r0 = j * blk
        x0, win, lane = _pool_window_sums(lambda d: uu[16 + r0 - d:16 + r0 - d + blk, :])
        pos = i * tq + r0 + lax.broadcasted_iota(jnp.int32, x0.shape, 0)
        cnt = jnp.minimum(_pool_window_len(lane), pos + 1).astype(F32)
        pooled = (win / cnt - x0).astype(BF16)
        pm = jnp.dot(pooled, poolw_ref[...], preferred_element_type=F32) * pscale_ref[...]
        mix[r0:r0 + blk, 0:POOL_WIDTH] = pm.astype(BF16)
        for g in range(N_KV_HEADS):
            qs = jnp.concatenate(
                [q_ref[r0:r0 + blk, HEAD_DIM * (GQA_GROUP * g + r):HEAD_DIM * (GQA_GROUP * g + r + 1)]
                 for r in range(GQA_GROUP)], axis=0)
            kh = kk[r0:r0 + 2 * blk, HEAD_DIM * g:HEAD_DIM * (g + 1)]
            vh = vv[r0:r0 + 2 * blk, HEAD_DIM * g:HEAD_DIM * (g + 1)]
            s = lax.dot_general(qs, kh, (((1,), (1,)), ((), ())),
                                preferred_element_type=F32)
            s = s + bias_ref[g]
            if j == 0:
                s = s + no_prev
            ps, dens = [], []
            for r in range(GQA_GROUP):
                sr = s[r * blk:(r + 1) * blk]
                sink = sink_ref[GQA_GROUP * g + r]
                m = jnp.maximum(jnp.max(sr, axis=-1, keepdims=True), sink)
                p = jnp.exp(sr - m)
                dens.append(jnp.sum(p, axis=-1, keepdims=True) + jnp.exp(sink - m))
                ps.append(p.astype(BF16))
            o = jnp.dot(jnp.concatenate(ps, axis=0), vh, preferred_element_type=F32)
            for r in range(GQA_GROUP):
                c0 = POOL_WIDTH + HEAD_DIM * (GQA_GROUP * g + r)
                mix[r0:r0 + blk, c0:c0 + HEAD_DIM] = (o[r * blk:(r + 1) * blk] / dens[r]).astype(BF16)

    o_ref[...] = x_ref[...] + jnp.dot(mix[...], wout_ref[...], preferred_element_type=F32)


def _prompt_mix(x, q, k, v, u, sinks, bias, poolw_bd, pscale, wout_b, *, batch, seq, tq):
    nb = seq // tq
    x3 = x.reshape(batch, seq, D_MODEL)
    q3 = q.reshape(batch, seq, ATTN_WIDTH)
    k3 = k.reshape(batch, seq, KV_WIDTH)
    v3 = v.reshape(batch, seq, KV_WIDTH)
    u3 = u.reshape(batch, seq, POOL_WIDTH)
    cur = lambda b, i: (b, i, 0)
    prev_kv = lambda b, i: (b, jnp.maximum(i * (tq // WINDOW) - 1, 0), 0)
    prev_u = lambda b, i: (b, jnp.maximum(i * (tq // 16) - 1, 0), 0)
    const2 = lambda b, i: (0, 0)
    const3 = lambda b, i: (0, 0, 0)
    out = pl.pallas_call(
        functools.partial(_prompt_mix_kernel, tq=tq),
        grid=(batch, nb),
        in_specs=[
            pl.BlockSpec(memory_space=pltpu.SMEM),
            pl.BlockSpec((None, tq, D_MODEL), cur),
            pl.BlockSpec((None, tq, ATTN_WIDTH), cur),
            pl.BlockSpec((None, tq, KV_WIDTH), cur),
            pl.BlockSpec((None, WINDOW, KV_WIDTH), prev_kv),
            pl.BlockSpec((None, tq, KV_WIDTH), cur),
            pl.BlockSpec((None, WINDOW, KV_WIDTH), prev_kv),
            pl.BlockSpec((None, tq, POOL_WIDTH), cur),
            pl.BlockSpec((None, 16, POOL_WIDTH), prev_u),
            pl.BlockSpec((N_KV_HEADS, GQA_GROUP * WINDOW, 2 * WINDOW), const3),
            pl.BlockSpec((POOL_WIDTH, POOL_WIDTH), const2),
            pl.BlockSpec((1, POOL_WIDTH), const2),
            pl.BlockSpec((D_MODEL, D_MODEL), const2),
        ],
        out_specs=pl.BlockSpec((None, tq, D_MODEL), cur),
        out_shape=jax.ShapeDtypeStruct((batch, seq, D_MODEL), F32),
        scratch_shapes=[
            pltpu.VMEM((tq + WINDOW, KV_WIDTH), BF16),
            pltpu.VMEM((tq + WINDOW, KV_WIDTH), BF16),
            pltpu.VMEM((tq + 16, POOL_WIDTH), F32),
            pltpu.VMEM((tq, D_MODEL), BF16),
        ],
        compiler_params=_params("arbitrary", "arbitrary"),
        name="prompt_mix",
    )(sinks, x3, q3, k3, k3, v3, v3, u3, u3, bias, poolw_bd, pscale, wout_b)
    return out.reshape(batch * seq, D_MODEL)


SAMPLE_ROWS = 32
SAMPLE_KEYS = 256


def _sample_mix_kernel(x_ref, q_ref, kn_ref, vn_ref, un_ref, kb_ref, vb_ref, pb_ref,
                       bias_ref, sinkcol_ref, poolw_ref, pscale_ref, wout_ref,
                       o_ref, ko_ref, vo_ref, po_ref,
                       kx, vx, px, pooled_s, mix, *, group, t_new, n_buf):
    kx[n_buf:, :] = jnp.zeros((SAMPLE_KEYS - n_buf, KV_WIDTH), F32)
    vx[n_buf:, :] = jnp.zeros((SAMPLE_KEYS - n_buf, KV_WIDTH), F32)

    def per_seq(n, carry):
        row0 = pl.multiple_of(n * t_new, t_new)
        kx[0:n_buf, :] = kb_ref[n]
        kx[n_buf:n_buf + t_new, :] = kn_ref[n]
        vx[0:n_buf, :] = vb_ref[n]
        vx[n_buf:n_buf + t_new, :] = vn_ref[n]
        ko_ref[n] = kx[t_new:t_new + n_buf, :]
        vo_ref[n] = vx[t_new:t_new + n_buf, :]
        px[1:16, :] = pb_ref[n]
        px[16:16 + t_new, :] = un_ref[n]
        po_ref[n] = px[16 + t_new - POOL_BUF:16 + t_new, :]
        x0, win, lane = _pool_window_sums(lambda d: px[16 - d:16 - d + t_new, :])
        cnt = _pool_window_len(lane).astype(F32)
        pooled_s[pl.ds(row0, t_new), :] = win / cnt - x0
        qn = q_ref[n]
        for g in range(N_KV_HEADS):
            heads = [qn[:, HEAD_DIM * (GQA_GROUP * g + r):HEAD_DIM * (GQA_GROUP * g + r + 1)]
                     for r in range(GQA_GROUP)]
            qs = jnp.concatenate(heads + [heads[-1]], axis=0).astype(BF16)
            kh = kx[:, HEAD_DIM * g:HEAD_DIM * (g + 1)].astype(BF16)
            vh = vx[:, HEAD_DIM * g:HEAD_DIM * (g + 1)].astype(BF16)
            s = lax.dot_general(qs, kh, (((1,), (1,)), ((), ())),
                                preferred_element_type=F32) + bias_ref[g]
            sink = sinkcol_ref[g][:, 0:1]
            m = jnp.maximum(jnp.max(s, axis=-1, keepdims=True), sink)
            p = jnp.exp(s - m)
            den = jnp.sum(p, axis=-1, keepdims=True) + jnp.exp(sink - m)
            o = jnp.dot(p.astype(BF16), vh, preferred_element_type=F32) / den
            for r in range(GQA_GROUP):
                c0 = POOL_WIDTH + HEAD_DIM * (GQA_GROUP * g + r)
                mix[pl.ds(row0, t_new), c0:c0 + HEAD_DIM] = o[r * t_new:(r + 1) * t_new]
        return carry

    lax.fori_loop(0, group, per_seq, 0)
    pm = jnp.dot(pooled_s[...].astype(BF16), poolw_ref[...], preferred_element_type=F32) * pscale_ref[...]
    mix[:, 0:POOL_WIDTH] = pm
    o_ref[...] = x_ref[...] + jnp.dot(mix[...].astype(BF16), wout_ref[...], preferred_element_type=F32)


def _sample_mix(x, q, k, v, u, kbuf, vbuf, pbuf, bias, sinkcol, poolw_bd, pscale, wout_b,
                *, n_seq, t_new, group):
    n_buf = kbuf.shape[1]
    q3 = q.reshape(n_seq, t_new, ATTN_WIDTH)
    k3 = k.reshape(n_seq, t_new, KV_WIDTH)
    v3 = v.reshape(n_seq, t_new, KV_WIDTH)
    u3 = u.reshape(n_seq, t_new, POOL_WIDTH)
    rows = group * t_new
    seq3 = lambda i: (i, 0, 0)
    row2 = lambda i: (i, 0)
    const2 = lambda i: (0, 0)
    const3 = lambda i: (0, 0, 0)
    return pl.pallas_call(
        functools.partial(_sample_mix_kernel, group=group, t_new=t_new, n_buf=n_buf),
        grid=(n_seq // group,),
        in_specs=[
            pl.BlockSpec((rows, D_MODEL), row2),
            pl.BlockSpec((group, t_new, ATTN_WIDTH), seq3),
            pl.BlockSpec((group, t_new, KV_WIDTH), seq3),
            pl.BlockSpec((group, t_new, KV_WIDTH), seq3),
            pl.BlockSpec((group, t_new, POOL_WIDTH), seq3),
            pl.BlockSpec((group, n_buf, KV_WIDTH), seq3),
            pl.BlockSpec((group, n_buf, KV_WIDTH), seq3),
            pl.BlockSpec((group, POOL_BUF, POOL_WIDTH), seq3),
            pl.BlockSpec((N_KV_HEADS, SAMPLE_ROWS, SAMPLE_KEYS), const3),
            pl.BlockSpec((N_KV_HEADS, SAMPLE_ROWS, LANES), const3),
            pl.BlockSpec((POOL_WIDTH, POOL_WIDTH), const2),
            pl.BlockSpec((1, POOL_WIDTH), const2),
            pl.BlockSpec((D_MODEL, D_MODEL), const2),
        ],
        out_specs=[
            pl.BlockSpec((rows, D_MODEL), row2),
            pl.BlockSpec((group, n_buf, KV_WIDTH), seq3),
            pl.BlockSpec((group, n_buf, KV_WIDTH), seq3),
            pl.BlockSpec((group, POOL_BUF, POOL_WIDTH), seq3),
        ],
        out_shape=[
            jax.ShapeDtypeStruct((n_seq * t_new, D_MODEL), F32),
            jax.ShapeDtypeStruct((n_seq, n_buf, KV_WIDTH), F32),
            jax.ShapeDtypeStruct((n_seq, n_buf, KV_WIDTH), F32),
            jax.ShapeDtypeStruct((n_seq, POOL_BUF, POOL_WIDTH), F32),
        ],
        scratch_shapes=[
            pltpu.VMEM((SAMPLE_KEYS, KV_WIDTH), F32),
            pltpu.VMEM((SAMPLE_KEYS, KV_WIDTH), F32),
            pltpu.VMEM((16 + t_new, POOL_WIDTH), F32),
            pltpu.VMEM((rows, POOL_WIDTH), F32),
            pltpu.VMEM((rows, D_MODEL), F32),
        ],
        compiler_params=_params("arbitrary"),
        name="sample_mix",
    )(x, q3, k3, v3, u3, kbuf, vbuf, pbuf, bias, sinkcol, poolw_bd, pscale, wout_b)


def _rms(xf, g):
    ms = jnp.mean(xf * xf, axis=-1, keepdims=True)
    return xf * lax.rsqrt(ms + EPS) * g


def _swiglu(h, wg, wu, wd):
    gate = jnp.dot(h, wg, preferred_element_type=F32)
    up = jnp.dot(h, wu, preferred_element_type=F32)
    act = (gate * jax.nn.sigmoid(gate) * up).astype(BF16)
    return jnp.dot(act, wd, preferred_element_type=F32)


def _ffn_kernel(x_ref, g_ref, wg_ref, wu_ref, wd_ref, o_ref, hb, acc):
    f = pl.program_id(1)

    @pl.when(f == 0)
    def _():
        hb[...] = _rms(x_ref[...], g_ref[...]).astype(BF16)
        acc[...] = jnp.zeros_like(acc)

    acc[...] += _swiglu(hb[...], wg_ref[...], wu_ref[...], wd_ref[...])

    @pl.when(f == pl.num_programs(1) - 1)
    def _():
        o_ref[...] = x_ref[...] + acc[...]


def _ffn(x, g, wg, wu, wd, *, tm, tf):
    n = x.shape[0]
    dff = wg.shape[1]
    row = lambda i, f: (i, 0)
    const = lambda i, f: (0, 0)
    return pl.pallas_call(
        _ffn_kernel,
        grid=(n // tm, dff // tf),
        in_specs=[
            pl.BlockSpec((tm, D_MODEL), row),
            pl.BlockSpec((1, D_MODEL), const),
            pl.BlockSpec((D_MODEL, tf), lambda i, f: (0, f)),
            pl.BlockSpec((D_MODEL, tf), lambda i, f: (0, f)),
            pl.BlockSpec((tf, D_MODEL), lambda i, f: (f, 0)),
        ],
        out_specs=pl.BlockSpec((tm, D_MODEL), row),
        out_shape=jax.ShapeDtypeStruct((n, D_MODEL), F32),
        scratch_shapes=[
            pltpu.VMEM((tm, D_MODEL), BF16),
            pltpu.VMEM((tm, D_MODEL), F32),
        ],
        compiler_params=_params("arbitrary", "arbitrary"),
        name="dense_ffn",
    )(x, g, wg, wu, wd)


MOE_TOK_TILE = 256
MOE_ROW_TILE = 256
FF_CHUNK = 1408


def _route_kernel(x_ref, g_ref, rhi_ref, rlo_ref, rt_ref):
    h = _rms(x_ref[...], g_ref[...])
    hi, lo = _split_bf16(h)
    lg = (jnp.dot(hi, rhi_ref[...], preferred_element_type=F32)
          + jnp.dot(lo, rhi_ref[...], preferred_element_type=F32)
          + jnp.dot(hi, rlo_ref[...], preferred_element_type=F32))
    lane = lax.broadcasted_iota(jnp.int32, lg.shape, 1)
    lg = jnp.where(lane < N_EXPERTS, lg, NEG_INF)
    m1 = jnp.max(lg, axis=-1, keepdims=True)
    i1 = jnp.min(jnp.where(lg == m1, lane, LANES), axis=-1, keepdims=True)
    lg2 = jnp.where(lane == i1, NEG_INF, lg)
    m2 = jnp.max(lg2, axis=-1, keepdims=True)
    i2 = jnp.min(jnp.where(lg2 == m2, lane, LANES), axis=-1, keepdims=True)
    e2 = jnp.exp(m2 - m1)
    den = 1.0 + e2
    rt_ref[...] = jnp.where(lane == 0, i1.astype(F32),
                            jnp.where(lane == 1, i2.astype(F32),
                                      jnp.where(lane == 2, 1.0 / den,
                                                jnp.where(lane == 3, e2 / den, 0.0))))


def _route(x, g, rhi, rlo, *, tm):
    n = x.shape[0]
    row = lambda i: (i, 0)
    const = lambda i: (0, 0)
    return pl.pallas_call(
        _route_kernel,
        grid=(n // tm,),
        in_specs=[
            pl.BlockSpec((tm, D_MODEL), row),
            pl.BlockSpec((1, D_MODEL), const),
            pl.BlockSpec((D_MODEL, LANES), const),
            pl.BlockSpec((D_MODEL, LANES), const),
        ],
        out_specs=pl.BlockSpec((tm, LANES), row),
        out_shape=jax.ShapeDtypeStruct((n, LANES), F32),
        compiler_params=_params("arbitrary"),
        name="moe_route",
    )(x, g, rhi, rlo)


def _row_copies_done(hbm_ref, sem, n_rows):
    pltpu.make_async_copy(hbm_ref.at[pl.ds(0, n_rows), :], hbm_ref.at[pl.ds(0, n_rows), :], sem).wait()


def _dispatch_kernel(pos_ref, x_ref, g_ref, xs_zero_ref, xs_ref, hbuf, sem, *, tm):
    del xs_zero_ref
    i = pl.program_id(0)
    slot = lax.rem(i, 2)

    @pl.when(i >= 2)
    def _():
        _row_copies_done(xs_ref, sem.at[slot], 2 * tm)

    hbuf[slot] = _rms(x_ref[...], g_ref[...])

    def issue(a, carry):
        tok = lax.shift_right_logical(a, 1)
        pltpu.make_async_copy(hbuf.at[slot, pl.ds(tok, 1), :],
                              xs_ref.at[pl.ds(pos_ref[0, a], 1), :],
                              sem.at[slot]).start()
        return carry

    lax.fori_loop(0, 2 * tm, issue, 0, unroll=8)

    @pl.when(i == pl.num_programs(0) - 1)
    def _():
        _row_copies_done(xs_ref, sem.at[slot], 2 * tm)

        @pl.when(i >= 1)
        def _():
            _row_copies_done(xs_ref, sem.at[1 - slot], 2 * tm)


def _dispatch(x, g, pos, xs_zero, *, tm):
    n = x.shape[0]
    return pl.pallas_call(
        functools.partial(_dispatch_kernel, tm=tm),
        grid=(n // tm,),
        in_specs=[
            pl.BlockSpec((None, 1, 2 * tm), lambda i: (i, 0, 0), memory_space=pltpu.SMEM),
            pl.BlockSpec((tm, D_MODEL), lambda i: (i, 0)),
            pl.BlockSpec((1, D_MODEL), lambda i: (0, 0)),
            pl.BlockSpec(memory_space=pl.ANY),
        ],
        out_specs=pl.BlockSpec(memory_space=pl.ANY),
        out_shape=jax.ShapeDtypeStruct(xs_zero.shape, F32),
        scratch_shapes=[
            pltpu.VMEM((2, tm, D_MODEL), F32),
            pltpu.SemaphoreType.DMA((2,)),
        ],
        input_output_aliases={3: 0},
        compiler_params=_params("arbitrary"),
        name="moe_dispatch",
    )(pos.reshape(n // tm, 1, 2 * tm), x, g, xs_zero)


def _expert_kernel(te_ref, tv_ref, x_ref, wg_ref, wu_ref, wd_ref, y_ref):
    t = pl.program_id(0)

    @pl.when(tv_ref[t] != 0)
    def _():
        h = x_ref[...].astype(BF16)
        acc = None
        for c0 in range(0, D_FF, FF_CHUNK):
            y = _swiglu(h, wg_ref[:, c0:c0 + FF_CHUNK], wu_ref[:, c0:c0 + FF_CHUNK],
                        wd_ref[c0:c0 + FF_CHUNK, :])
            acc = y if acc is None else acc + y
        y_ref[...] = acc

    @pl.when(tv_ref[t] == 0)
    def _():
        y_ref[...] = jnp.zeros_like(y_ref)


def _experts(xs, tile_expert, tile_valid, wg, wu, wd, *, tr):
    rows = xs.shape[0]
    wmap = lambda t, te, tv: (te[t], 0, 0)
    grid_spec = pltpu.PrefetchScalarGridSpec(
        num_scalar_prefetch=2,
        grid=(rows // tr,),
        in_specs=[
            pl.BlockSpec((tr, D_MODEL), lambda t, te, tv: (t, 0)),
            pl.BlockSpec((None, D_MODEL, D_FF), wmap),
            pl.BlockSpec((None, D_MODEL, D_FF), wmap),
            pl.BlockSpec((None, D_FF, D_MODEL), wmap),
        ],
        out_specs=pl.BlockSpec((tr, D_MODEL), lambda t, te, tv: (t, 0)),
    )
    return pl.pallas_call(
        _expert_kernel,
        grid_spec=grid_spec,
        out_shape=jax.ShapeDtypeStruct((rows, D_MODEL), F32),
        compiler_params=_params("arbitrary"),
        name="moe_experts",
    )(tile_expert, tile_valid, xs, wg, wu, wd)


def _combine_kernel(pos_ref, posn_ref, x_ref, rt_ref, ys_ref, o_ref, ybuf, sem, *, tm):
    i = pl.program_id(0)
    last = pl.num_programs(0) - 1
    slot = lax.rem(i, 2)

    def gather(p_ref, s):
        def issue(a, carry):
            tok = lax.shift_right_logical(a, 1)
            k = lax.bitwise_and(a, 1)
            pltpu.make_async_copy(ys_ref.at[pl.ds(p_ref[0, a], 1), :],
                                  ybuf.at[s, k, pl.ds(tok, 1), :],
                                  sem.at[s]).start()
            return carry
        lax.fori_loop(0, 2 * tm, issue, 0, unroll=8)

    @pl.when(i == 0)
    def _():
        gather(pos_ref, slot)

    @pl.when(i < last)
    def _():
        gather(posn_ref, 1 - slot)

    _row_copies_done(ys_ref, sem.at[slot], 2 * tm)
    rt = rt_ref[...]
    o_ref[...] = x_ref[...] + (rt[:, 2:3] * ybuf[slot, 0] + rt[:, 3:4] * ybuf[slot, 1])


def _combine(x, rt, pos, ys, *, tm):
    n = x.shape[0]
    nt = n // tm
    pos3 = pos.reshape(nt, 1, 2 * tm)
    row = lambda i: (i, 0)
    return pl.pallas_call(
        functools.partial(_combine_kernel, tm=tm),
        grid=(nt,),
        in_specs=[
            pl.BlockSpec((None, 1, 2 * tm), lambda i: (i, 0, 0), memory_space=pltpu.SMEM),
            pl.BlockSpec((None, 1, 2 * tm), lambda i: (jnp.minimum(i + 1, nt - 1), 0, 0),
                         memory_space=pltpu.SMEM),
            pl.BlockSpec((tm, D_MODEL), row),
            pl.BlockSpec((tm, LANES), row),
            pl.BlockSpec(memory_space=pl.ANY),
        ],
        out_specs=pl.BlockSpec((tm, D_MODEL), row),
        out_shape=jax.ShapeDtypeStruct((n, D_MODEL), F32),
        scratch_shapes=[
            pltpu.VMEM((2, 2, tm, D_MODEL), F32),
            pltpu.SemaphoreType.DMA((2,)),
        ],
        compiler_params=_params("arbitrary"),
        name="moe_combine",
    )(pos3, pos3, x, rt, ys)


def _moe(x, g, rhi, rlo, wg, wu, wd):
    n = x.shape[0]
    tm, tr = MOE_TOK_TILE, MOE_ROW_TILE
    rt = _route(x, g, rhi, rlo, tm=tm)
    expert = rt[:, :2].astype(jnp.int32).reshape(2 * n)
    onehot = (expert[:, None] == jnp.arange(N_EXPERTS, dtype=jnp.int32)[None, :]).astype(jnp.int32)
    csum = jnp.cumsum(onehot, axis=0)
    counts = csum[-1]
    group = ((counts + tr - 1) // tr) * tr
    group_end = jnp.cumsum(group)
    pos = jnp.sum(onehot * (csum - 1 + (group_end - group)[None, :]), axis=1).astype(jnp.int32)
    n_rows = 2 * n + N_EXPERTS * tr
    tile_start = jnp.arange(n_rows // tr, dtype=jnp.int32) * tr
    tile_expert = jnp.minimum(jnp.sum(tile_start[:, None] >= group_end[None, :], axis=1),
                              N_EXPERTS - 1).astype(jnp.int32)
    tile_valid = (tile_start < group_end[-1]).astype(jnp.int32)

    xs = _dispatch(x, g, pos, jnp.zeros((n_rows, D_MODEL), F32), tm=tm)
    ys = _experts(xs, tile_expert, tile_valid, wg, wu, wd, tr=tr)
    return _combine(x, rt, pos, ys, tm=tm)


def _head_selectors():
    sel = np.zeros((QK_WIDTH, LANES), np.float32)
    sel[np.arange(QK_WIDTH), np.arange(QK_WIDTH) // HEAD_DIM] = 1.0
    return jnp.asarray(sel, BF16), jnp.asarray(sel.T.copy(), BF16)


def _prompt_bias():
    slopes = _alibi_slopes(N_Q_HEADS)
    qi = np.arange(WINDOW)[:, None]
    kj = np.arange(2 * WINDOW)[None, :]
    dist = (qi + WINDOW - kj).astype(np.float32)
    valid = (dist >= 0) & (dist < WINDOW)
    out = np.empty((N_KV_HEADS, GQA_GROUP * WINDOW, 2 * WINDOW), np.float32)
    for g in range(N_KV_HEADS):
        for r in range(GQA_GROUP):
            b = np.where(valid, -(slopes[GQA_GROUP * g + r] * dist), -np.inf)
            out[g, r * WINDOW:(r + 1) * WINDOW] = b
    return jnp.asarray(out)


def _sample_bias(t_new, n_buf):
    slopes = _alibi_slopes(N_Q_HEADS)
    t = np.arange(t_new)[:, None]
    j = np.arange(SAMPLE_KEYS)[None, :]
    dist = (t + n_buf - j).astype(np.float32)
    valid = (dist >= 0) & (dist < WINDOW) & (j < n_buf + t_new)
    out = np.empty((N_KV_HEADS, SAMPLE_ROWS, SAMPLE_KEYS), np.float32)
    for g in range(N_KV_HEADS):
        for r in range(GQA_GROUP + 1):
            h = GQA_GROUP * g + min(r, GQA_GROUP - 1)
            out[g, r * t_new:(r + 1) * t_new] = np.where(valid, -(slopes[h] * dist), -np.inf)
    return jnp.asarray(out)


def _block_diag_pool(pool_w_l):
    bd = jnp.zeros((POOL_WIDTH, POOL_WIDTH), F32)
    for g in range(len(POOL_WINDOWS)):
        sl = slice(g * POOL_GROUP_DIM, (g + 1) * POOL_GROUP_DIM)
        bd = bd.at[sl, sl].set(pool_w_l[g])
    return bd.astype(BF16)


def kernel(x_prompt, x_sample, state_pool, state_win_k, state_win_v, norm_mix, w_in, q_norm, k_norm,
           attn_sinks, pool_w, pool_scale, w_out, norm_ffn, ffn_w_gate, ffn_w_up, ffn_w_down,
           moe_router, moe_w_gate, moe_w_up, moe_w_down):
    batch, seq, _ = x_prompt.shape
    n_seq, t_new, _ = x_sample.shape
    n_buf = state_win_k.shape[2]
    n_keep = min(WINDOW, seq)
    assert t_new == 8 and n_buf == WINDOW and SAMPLE_ROWS == (GQA_GROUP + 1) * t_new

    sel, selt = _head_selectors()
    bias_p = _prompt_bias()
    bias_s = _sample_bias(t_new, n_buf)

    xp = x_prompt.reshape(batch * seq, D_MODEL)
    xs = x_sample.reshape(n_seq * t_new, D_MODEL)
    pool_p, k_p, v_p, pool_s, k_s, v_s = [], [], [], [], [], []
    for l in range(DEPTH):
        g_mix = norm_mix[l].reshape(1, D_MODEL)
        g_ffn = norm_ffn[l].reshape(1, D_MODEL)
        w_in_b = w_in[l].astype(BF16)
        w_out_b = w_out[l].astype(BF16)
        gain = jnp.concatenate([jnp.tile(q_norm[l], N_Q_HEADS) * (HEAD_DIM ** -0.5),
                                jnp.tile(k_norm[l], N_KV_HEADS)]).reshape(1, QK_WIDTH)
        poolw_bd = _block_diag_pool(pool_w[l])
        pscale = pool_scale[l].reshape(1, POOL_WIDTH)
        sinks = attn_sinks[l].astype(F32)
        sink_rows = jnp.repeat(
            jnp.concatenate([sinks.reshape(N_KV_HEADS, GQA_GROUP), sinks.reshape(N_KV_HEADS, GQA_GROUP)[:, -1:]], axis=1),
            t_new, axis=1)
        sinkcol = jnp.broadcast_to(sink_rows[:, :, None], (N_KV_HEADS, SAMPLE_ROWS, LANES))

        i = l // 2
        if l % 2 == 0:
            wg = ffn_w_gate[i].astype(BF16)
            wu = ffn_w_up[i].astype(BF16)
            wd = ffn_w_down[i].astype(BF16)
            channel_mix = lambda x: _ffn(x, g_ffn, wg, wu, wd, tm=512, tf=1408)
        else:
            wg = moe_w_gate[i].astype(BF16)
            wu = moe_w_up[i].astype(BF16)
            wd = moe_w_down[i].astype(BF16)
            rpad = jnp.pad(moe_router[i], ((0, 0), (0, LANES - N_EXPERTS)))
            rhi, rlo = _split_bf16(rpad)
            channel_mix = lambda x: _moe(x, g_ffn, rhi, rlo, wg, wu, wd)

        u, q, k, v = _project(xp, g_mix, w_in_b, sel, selt, gain, tm=512, q_dtype=BF16)
        xp = _prompt_mix(xp, q, k, v, u, sinks, bias_p, poolw_bd, pscale, w_out_b,
                         batch=batch, seq=seq, tq=512)
        xp = channel_mix(xp)
        pool_p.append(u.reshape(batch, seq, POOL_WIDTH)[:, seq - POOL_BUF:])
        k_p.append(k.reshape(batch, seq, N_KV_HEADS, HEAD_DIM)[:, seq - n_keep:])
        v_p.append(v.reshape(batch, seq, N_KV_HEADS, HEAD_DIM)[:, seq - n_keep:])

        u, q, k, v = _project(xs, g_mix, w_in_b, sel, selt, gain, tm=512, q_dtype=F32)
        xs, ko, vo, po = _sample_mix(
            xs, q, k, v, u,
            state_win_k[l].reshape(n_seq, n_buf, KV_WIDTH),
            state_win_v[l].reshape(n_seq, n_buf, KV_WIDTH),
            state_pool[l], bias_s, sinkcol, poolw_bd, pscale, w_out_b,
            n_seq=n_seq, t_new=t_new, group=16)
        xs = channel_mix(xs)
        pool_s.append(po)
        k_s.append(ko.reshape(n_seq, n_buf, N_KV_HEADS, HEAD_DIM))
        v_s.append(vo.reshape(n_seq, n_buf, N_KV_HEADS, HEAD_DIM))

    return (xp.reshape(batch, seq, D_MODEL), xs.reshape(n_seq, t_new, D_MODEL),
            jnp.stack(pool_p), jnp.stack(k_p), jnp.stack(v_p),
            jnp.stack(pool_s), jnp.stack(k_s), jnp.stack(v_s))
```

```python
import functools
import math

import numpy as np
import jax
import jax.numpy as jnp
from jax import lax
from jax.experimental import pallas as pl
from jax.experimental.pallas import tpu as pltpu

F32 = jnp.float32
BF16 = jnp.bfloat16

D_MODEL = 1024
DEPTH = 2
POOL_WIDTH = 256
POOL_GROUP_DIM = 64
POOL_WINDOWS = (2, 4, 8, 16)
POOL_BUF = 15
HEAD_DIM = 64
N_Q_HEADS = 12
N_KV_HEADS = 4
GQA_GROUP = 3
ATTN_WIDTH = 768
KV_WIDTH = 256
QK_WIDTH = ATTN_WIDTH + KV_WIDTH
IN_WIDTH = 1536
WINDOW = 128
D_FF = 2816
N_EXPERTS = 8
EPS = 1e-6
NEG_INF = float("-inf")

LANES = 128
VMEM_LIMIT = 56 * 1024 * 1024


def _alibi_slopes(n):
    def pow2_slopes(m):
        start = 2.0 ** (-8.0 / m)
        return [start ** (i + 1) for i in range(m)]
    if float(math.log2(n)).is_integer():
        s = pow2_slopes(n)
    else:
        c = 2 ** int(math.floor(math.log2(n)))
        s = pow2_slopes(c) + pow2_slopes(2 * c)[0::2][: n - c]
    return np.array(s, dtype=np.float32)


def _split_bf16(x):
    hi = x.astype(BF16)
    lo = (x - hi.astype(F32)).astype(BF16)
    return hi, lo


def _params(*sem):
    return pltpu.CompilerParams(dimension_semantics=sem, vmem_limit_bytes=VMEM_LIMIT)


def _proj_kernel(x_ref, g_ref, w_ref, sel_ref, selt_ref, gain_ref,
                 u_ref, q_ref, k_ref, v_ref):
    xf = x_ref[...]
    ms = jnp.mean(xf * xf, axis=-1, keepdims=True)
    h = (xf * lax.rsqrt(ms + EPS) * g_ref[...]).astype(BF16)
    proj = jnp.dot(h, w_ref[...], preferred_element_type=F32)
    u_ref[...] = proj[:, :POOL_WIDTH]
    v_ref[...] = proj[:, POOL_WIDTH + QK_WIDTH:]
    qk = proj[:, POOL_WIDTH:POOL_WIDTH + QK_WIDTH]
    hi, lo = _split_bf16(qk * qk)
    ss = (jnp.dot(hi, sel_ref[...], preferred_element_type=F32)
          + jnp.dot(lo, sel_ref[...], preferred_element_type=F32))
    scale = lax.rsqrt(ss * (1.0 / HEAD_DIM) + EPS)
    shi, slo = _split_bf16(scale)
    full = (jnp.dot(shi, selt_ref[...], preferred_element_type=F32)
            + jnp.dot(slo, selt_ref[...], preferred_element_type=F32))
    qkn = qk * full * gain_ref[...]
    q_ref[...] = qkn[:, :ATTN_WIDTH].astype(q_ref.dtype)
    k_ref[...] = qkn[:, ATTN_WIDTH:]


def _project(x, g, w_in_b, sel, selt, gain, *, tm, q_dtype):
    n = x.shape[0]
    const = lambda i: (0, 0)
    row = lambda i: (i, 0)
    return pl.pallas_call(
        _proj_kernel,
        grid=(n // tm,),
        in_specs=[
            pl.BlockSpec((tm, D_MODEL), row),
            pl.BlockSpec((1, D_MODEL), const),
            pl.BlockSpec((D_MODEL, IN_WIDTH), const),
            pl.BlockSpec((QK_WIDTH, LANES), const),
            pl.BlockSpec((LANES, QK_WIDTH), const),
            pl.BlockSpec((1, QK_WIDTH), const),
        ],
        out_specs=[
            pl.BlockSpec((tm, POOL_WIDTH), row),
            pl.BlockSpec((tm, ATTN_WIDTH), row),
            pl.BlockSpec((tm, KV_WIDTH), row),
            pl.BlockSpec((tm, KV_WIDTH), row),
        ],
        out_shape=[
            jax.ShapeDtypeStruct((n, POOL_WIDTH), F32),
            jax.ShapeDtypeStruct((n, ATTN_WIDTH), q_dtype),
            jax.ShapeDtypeStruct((n, KV_WIDTH), F32),
            jax.ShapeDtypeStruct((n, KV_WIDTH), F32),
        ],
        compiler_params=_params("arbitrary"),
        name="norm_proj",
    )(x, g, w_in_b, sel, selt, gain)


def _pool_window_sums(load_shifted, lane_axis=1):
    x0 = load_shifted(0)
    acc = x0 + load_shifted(1)
    s2 = acc
    for d in range(2, 4):
        acc = acc + load_shifted(d)
    s4 = acc
    for d in range(4, 8):
        acc = acc + load_shifted(d)
    s8 = acc
    for d in range(8, 16):
        acc = acc + load_shifted(d)
    s16 = acc
    lane = lax.broadcasted_iota(jnp.int32, x0.shape, lane_axis)
    win = jnp.where(lane < 64, s2, jnp.where(lane < 128, s4, jnp.where(lane < 192, s8, s16)))
    return x0, win, lane


def _pool_window_len(lane):
    return jnp.where(lane < 64, 2, jnp.where(lane < 128, 4, jnp.where(lane < 192, 8, 16)))


def _prompt_mix_kernel(sink_ref, x_ref, q_ref, kc_ref, kp_ref, vc_ref, vp_ref, uc_ref, up_ref,
                       bias_ref, poolw_ref, pscale_ref, wout_ref,
                       o_ref, kk, vv, uu, mix, *, tq):
    i = pl.program_id(1)
    first = i == 0
    blk = WINDOW
    kk[0:blk] = kp_ref[...].astype(BF16)
    kk[blk:] = kc_ref[...].astype(BF16)
    vv[0:blk] = vp_ref[...].astype(BF16)
    vv[blk:] = vc_ref[...].astype(BF16)
    uu[0:16] = jnp.where(first, 0.0, up_ref[...])
    uu[16:] = uc_ref[...]

    col = lax.broadcasted_iota(jnp.int32, (1, 2 * blk), 1)
    no_prev = jnp.where(jnp.logical_and(first, col < blk), NEG_INF, 0.0)

    for j in range(tq // blk):
        r0 = j * blk
        x0, win, lane = _pool_window_sums(lambda d: uu[16 + r0 - d:16 + r0 - d + blk, :])
        pos = i * tq + r0 + lax.broadcasted_iota(jnp.int32, x0.shape, 0)
        cnt = jnp.minimum(_pool_window_len(lane), pos + 1).astype(F32)
        pooled = (win / cnt - x0).astype(BF16)
        pm = jnp.dot(pooled, poolw_ref[...], preferred_element_type=F32) * pscale_ref[...]
        mix[r0:r0 + blk, 0:POOL_WIDTH] = pm.astype(BF16)
        for g in range(N_KV_HEADS):
            qs = jnp.concatenate(
                [q_ref[r0:r0 + blk, HEAD_DIM * (GQA_GROUP * g + r):HEAD_DIM * (GQA_GROUP * g + r + 1)]
                 for r in range(GQA_GROUP)], axis=0)
            kh = kk[r0:r0 + 2 * blk, HEAD_DIM * g:HEAD_DIM * (g + 1)]
            vh = vv[r0:r0 + 2 * blk, HEAD_DIM * g:HEAD_DIM * (g + 1)]
            s = lax.dot_general(qs, kh, (((1,), (1,)), ((), ())),
                                preferred_element_type=F32)
            s = s + bias_ref[g]
            if j == 0:
                s = s + no_prev
            ps, dens = [], []
            for r in range(GQA_GROUP):
                sr = s[r * blk:(r + 1) * blk]
                sink = sink_ref[GQA_GROUP * g + r]
                m = jnp.maximum(jnp.max(sr, axis=-1, keepdims=True), sink)
                p = jnp.exp(sr - m)
                dens.append(jnp.sum(p, axis=-1, keepdims=True) + jnp.exp(sink - m))
                ps.append(p.astype(BF16))
            o = jnp.dot(jnp.concatenate(ps, axis=0), vh, preferred_element_type=F32)
            for r in range(GQA_GROUP):
                c0 = POOL_WIDTH + HEAD_DIM * (GQA_GROUP * g + r)
                mix[r0:r0 + blk, c0:c0 + HEAD_DIM] = (o[r * blk:(r + 1) * blk] / dens[r]).astype(BF16)

    o_ref[...] = x_ref[...] + jnp.dot(mix[...], wout_ref[...], preferred_element_type=F32)


def _prompt_mix(x, q, k, v, u, sinks, bias, poolw_bd, pscale, wout_b, *, batch, seq, tq):
    nb = seq // tq
    x3 = x.reshape(batch, seq, D_MODEL)
    q3 = q.reshape(batch, seq, ATTN_WIDTH)
    k3 = k.reshape(batch, seq, KV_WIDTH)
    v3 = v.reshape(batch, seq, KV_WIDTH)
    u3 = u.reshape(batch, seq, POOL_WIDTH)
    cur = lambda b, i: (b, i, 0)
    prev_kv = lambda b, i: (b, jnp.maximum(i * (tq // WINDOW) - 1, 0), 0)
    prev_u = lambda b, i: (b, jnp.maximum(i * (tq // 16) - 1, 0), 0)
    const2 = lambda b, i: (0, 0)
    const3 = lambda b, i: (0, 0, 0)
    out = pl.pallas_call(
        functools.partial(_prompt_mix_kernel, tq=tq),
        grid=(batch, nb),
        in_specs=[
            pl.BlockSpec(memory_space=pltpu.SMEM),
            pl.BlockSpec((None, tq, D_MODEL), cur),
            pl.BlockSpec((None, tq, ATTN_WIDTH), cur),
            pl.BlockSpec((None, tq, KV_WIDTH), cur),
            pl.BlockSpec((None, WINDOW, KV_WIDTH), prev_kv),
            pl.BlockSpec((None, tq, KV_WIDTH), cur),
            pl.BlockSpec((None, WINDOW, KV_WIDTH), prev_kv),
            pl.BlockSpec((None, tq, POOL_WIDTH), cur),
            pl.BlockSpec((None, 16, POOL_WIDTH), prev_u),
            pl.BlockSpec((N_KV_HEADS, GQA_GROUP * WINDOW, 2 * WINDOW), const3),
            pl.BlockSpec((POOL_WIDTH, POOL_WIDTH), const2),
            pl.BlockSpec((1, POOL_WIDTH), const2),
            pl.BlockSpec((D_MODEL, D_MODEL), const2),
        ],
        out_specs=pl.BlockSpec((None, tq, D_MODEL), cur),
        out_shape=jax.ShapeDtypeStruct((batch, seq, D_MODEL), F32),
        scratch_shapes=[
            pltpu.VMEM((tq + WINDOW, KV_WIDTH), BF16),
            pltpu.VMEM((tq + WINDOW, KV_WIDTH), BF16),
            pltpu.VMEM((tq + 16, POOL_WIDTH), F32),
            pltpu.VMEM((tq, D_MODEL), BF16),
        ],
        compiler_params=_params("arbitrary", "arbitrary"),
        name="prompt_mix",
    )(sinks, x3, q3, k3, k3, v3, v3, u3, u3, bias, poolw_bd, pscale, wout_b)
    return out.reshape(batch * seq, D_MODEL)


SAMPLE_ROWS = 32
SAMPLE_KEYS = 256


def _sample_mix_kernel(x_ref, q_ref, kn_ref, vn_ref, un_ref, kb_ref, vb_ref, pb_ref,
                       bias_ref, sinkcol_ref, poolw_ref, pscale_ref, wout_ref,
                       o_ref, ko_ref, vo_ref, po_ref,
                       kx, vx, px, mix, *, group, t_new, n_buf):
    rows = group * t_new
    ko_ref[:, 0:n_buf - t_new, :] = kb_ref[:, t_new:n_buf, :]
    ko_ref[:, n_buf - t_new:n_buf, :] = kn_ref[...]
    vo_ref[:, 0:n_buf - t_new, :] = vb_ref[:, t_new:n_buf, :]
    vo_ref[:, n_buf - t_new:n_buf, :] = vn_ref[...]
    pad8 = jnp.zeros((group, 16 - t_new, KV_WIDTH), F32)
    tail = jnp.zeros((group, SAMPLE_KEYS - n_buf - 16, KV_WIDTH), BF16)
    kx[:, 0:n_buf, :] = kb_ref[...].astype(BF16)
    kx[:, n_buf:n_buf + 16, :] = jnp.concatenate([kn_ref[...], pad8], axis=1).astype(BF16)
    kx[:, n_buf + 16:, :] = tail
    vx[:, 0:n_buf, :] = vb_ref[...].astype(BF16)
    vx[:, n_buf:n_buf + 16, :] = jnp.concatenate([vn_ref[...], pad8], axis=1).astype(BF16)
    vx[:, n_buf + 16:, :] = tail

    px[:, 1:16, :] = pb_ref[...]
    px[:, 16:16 + t_new, :] = un_ref[...]
    po_ref[...] = px[:, 16 + t_new - POOL_BUF:16 + t_new, :]
    x0, win, lane = _pool_window_sums(lambda d: px[:, 16 - d:16 - d + t_new, :], lane_axis=2)
    cnt = _pool_window_len(lane).astype(F32)
    pooled = (win / cnt - x0).reshape(rows, POOL_WIDTH).astype(BF16)
    pm = jnp.dot(pooled, poolw_ref[...], preferred_element_type=F32) * pscale_ref[...]
    mix[:, :, 0:POOL_WIDTH] = pm.reshape(group, t_new, POOL_WIDTH)

    q3 = q_ref[...]
    for g in range(N_KV_HEADS):
        heads = [q3[:, :, HEAD_DIM * (GQA_GROUP * g + r):HEAD_DIM * (GQA_GROUP * g + r + 1)]
                 for r in range(GQA_GROUP)]
        qs = jnp.concatenate(heads + [heads[-1]], axis=1).astype(BF16)
        kh = kx[:, :, HEAD_DIM * g:HEAD_DIM * (g + 1)]
        vh = vx[:, :, HEAD_DIM * g:HEAD_DIM * (g + 1)]
        s = jnp.einsum("gqd,gkd->gqk", qs, kh, preferred_element_type=F32) + bias_ref[g][None]
        sink = sinkcol_ref[g][:, 0:1][None]
        m = jnp.maximum(jnp.max(s, axis=-1, keepdims=True), sink)
        p = jnp.exp(s - m)
        den = jnp.sum(p, axis=-1, keepdims=True) + jnp.exp(sink - m)
        o = jnp.einsum("gqk,gkd->gqd", p.astype(BF16), vh, preferred_element_type=F32) / den
        for r in range(GQA_GROUP):
            c0 = POOL_WIDTH + HEAD_DIM * (GQA_GROUP * g + r)
            mix[:, :, c0:c0 + HEAD_DIM] = o[:, r * t_new:(r + 1) * t_new, :]

    mixb = mix[...].reshape(rows, D_MODEL).astype(BF16)
    o_ref[...] = x_ref[...] + jnp.dot(mixb, wout_ref[...], preferred_element_type=F32)


def _sample_mix(x, q, k, v, u, kbuf, vbuf, pbuf, bias, sinkcol, poolw_bd, pscale, wout_b,
                *, n_seq, t_new, group):
    n_buf = kbuf.shape[1]
    q3 = q.reshape(n_seq, t_new, ATTN_WIDTH)
    k3 = k.reshape(n_seq, t_new, KV_WIDTH)
    v3 = v.reshape(n_seq, t_new, KV_WIDTH)
    u3 = u.reshape(n_seq, t_new, POOL_WIDTH)
    rows = group * t_new
    seq3 = lambda i: (i, 0, 0)
    row2 = lambda i: (i, 0)
    const2 = lambda i: (0, 0)
    const3 = lambda i: (0, 0, 0)
    return pl.pallas_call(
        functools.partial(_sample_mix_kernel, group=group, t_new=t_new, n_buf=n_buf),
        grid=(n_seq // group,),
        in_specs=[
            pl.BlockSpec((rows, D_MODEL), row2),
            pl.BlockSpec((group, t_new, ATTN_WIDTH), seq3),
            pl.BlockSpec((group, t_new, KV_WIDTH), seq3),
            pl.BlockSpec((group, t_new, KV_WIDTH), seq3),
            pl.BlockSpec((group, t_new, POOL_WIDTH), seq3),
            pl.BlockSpec((group, n_buf, KV_WIDTH), seq3),
            pl.BlockSpec((group, n_buf, KV_WIDTH), seq3),
            pl.BlockSpec((group, POOL_BUF, POOL_WIDTH), seq3),
            pl.BlockSpec((N_KV_HEADS, SAMPLE_ROWS, SAMPLE_KEYS), const3),
            pl.BlockSpec((N_KV_HEADS, SAMPLE_ROWS, LANES), const3),
            pl.BlockSpec((POOL_WIDTH, POOL_WIDTH), const2),
            pl.BlockSpec((1, POOL_WIDTH), const2),
            pl.BlockSpec((D_MODEL, D_MODEL), const2),
        ],
        out_specs=[
            pl.BlockSpec((rows, D_MODEL), row2),
            pl.BlockSpec((group, n_buf, KV_WIDTH), seq3),
            pl.BlockSpec((group, n_buf, KV_WIDTH), seq3),
            pl.BlockSpec((group, POOL_BUF, POOL_WIDTH), seq3),
        ],
        out_shape=[
            jax.ShapeDtypeStruct((n_seq * t_new, D_MODEL), F32),
            jax.ShapeDtypeStruct((n_seq, n_buf, KV_WIDTH), F32),
            jax.ShapeDtypeStruct((n_seq, n_buf, KV_WIDTH), F32),
            jax.ShapeDtypeStruct((n_seq, POOL_BUF, POOL_WIDTH), F32),
        ],
        scratch_shapes=[
            pltpu.VMEM((group, SAMPLE_KEYS, KV_WIDTH), BF16),
            pltpu.VMEM((group, SAMPLE_KEYS, KV_WIDTH), BF16),
            pltpu.VMEM((group, 16 + t_new, POOL_WIDTH), F32),
            pltpu.VMEM((group, t_new, D_MODEL), F32),
        ],
        compiler_params=_params("arbitrary"),
        name="sample_mix",
    )(x, q3, k3, v3, u3, kbuf, vbuf, pbuf, bias, sinkcol, poolw_bd, pscale, wout_b)


def _rms(xf, g):
    ms = jnp.mean(xf * xf, axis=-1, keepdims=True)
    return xf * lax.rsqrt(ms + EPS) * g


def _swiglu(h, wg, wu, wd):
    gate = jnp.dot(h, wg, preferred_element_type=F32)
    up = jnp.dot(h, wu, preferred_element_type=F32)
    act = (gate * jax.nn.sigmoid(gate) * up).astype(BF16)
    return jnp.dot(act, wd, preferred_element_type=F32)


def _ffn_kernel(x_ref, g_ref, wg_ref, wu_ref, wd_ref, o_ref, hb, acc):
    f = pl.program_id(1)

    @pl.when(f == 0)
    def _():
        hb[...] = _rms(x_ref[...], g_ref[...]).astype(BF16)
        acc[...] = jnp.zeros_like(acc)

    acc[...] += _swiglu(hb[...], wg_ref[...], wu_ref[...], wd_ref[...])

    @pl.when(f == pl.num_programs(1) - 1)
    def _():
        o_ref[...] = x_ref[...] + acc[...]


def _ffn(x, g, wg, wu, wd, *, tm, tf):
    n = x.shape[0]
    dff = wg.shape[1]
    row = lambda i, f: (i, 0)
    const = lambda i, f: (0, 0)
    return pl.pallas_call(
        _ffn_kernel,
        grid=(n // tm, dff // tf),
        in_specs=[
            pl.BlockSpec((tm, D_MODEL), row),
            pl.BlockSpec((1, D_MODEL), const),
            pl.BlockSpec((D_MODEL, tf), lambda i, f: (0, f)),
            pl.BlockSpec((D_MODEL, tf), lambda i, f: (0, f)),
            pl.BlockSpec((tf, D_MODEL), lambda i, f: (f, 0)),
        ],
        out_specs=pl.BlockSpec((tm, D_MODEL), row),
        out_shape=jax.ShapeDtypeStruct((n, D_MODEL), F32),
        scratch_shapes=[
            pltpu.VMEM((tm, D_MODEL), BF16),
            pltpu.VMEM((tm, D_MODEL), F32),
        ],
        compiler_params=_params("arbitrary", "arbitrary"),
        name="dense_ffn",
    )(x, g, wg, wu, wd)


MOE_TOK_TILE = 256
MOE_ROW_TILE = 256
FF_CHUNK = 1408


def _route_kernel(x_ref, g_ref, rhi_ref, rlo_ref, rt_ref):
    h = _rms(x_ref[...], g_ref[...])
    hi, lo = _split_bf16(h)
    lg = (jnp.dot(hi, rhi_ref[...], preferred_element_type=F32)
          + jnp.dot(lo, rhi_ref[...], preferred_element_type=F32)
          + jnp.dot(hi, rlo_ref[...], preferred_element_type=F32))
    lane = lax.broadcasted_iota(jnp.int32, lg.shape, 1)
    lg = jnp.where(lane < N_EXPERTS, lg, NEG_INF)
    m1 = jnp.max(lg, axis=-1, keepdims=True)
    i1 = jnp.min(jnp.where(lg == m1, lane, LANES), axis=-1, keepdims=True)
    lg2 = jnp.where(lane == i1, NEG_INF, lg)
    m2 = jnp.max(lg2, axis=-1, keepdims=True)
    i2 = jnp.min(jnp.where(lg2 == m2, lane, LANES), axis=-1, keepdims=True)
    e2 = jnp.exp(m2 - m1)
    den = 1.0 + e2
    rt_ref[...] = jnp.where(lane == 0, i1.astype(F32),
                            jnp.where(lane == 1, i2.astype(F32),
                                      jnp.where(lane == 2, 1.0 / den,
                                                jnp.where(lane == 3, e2 / den, 0.0))))


def _route(x, g, rhi, rlo, *, tm):
    n = x.shape[0]
    row = lambda i: (i, 0)
    const = lambda i: (0, 0)
    return pl.pallas_call(
        _route_kernel,
        grid=(n // tm,),
        in_specs=[
            pl.BlockSpec((tm, D_MODEL), row),
            pl.BlockSpec((1, D_MODEL), const),
            pl.BlockSpec((D_MODEL, LANES), const),
            pl.BlockSpec((D_MODEL, LANES), const),
        ],
        out_specs=pl.BlockSpec((tm, LANES), row),
        out_shape=jax.ShapeDtypeStruct((n, LANES), F32),
        compiler_params=_params("arbitrary"),
        name="moe_route",
    )(x, g, rhi, rlo)


def _row_copies_done(hbm_ref, sem, n_rows):
    pltpu.make_async_copy(hbm_ref.at[pl.ds(0, n_rows), :], hbm_ref.at[pl.ds(0, n_rows), :], sem).wait()


def _dispatch_kernel(pos_ref, x_ref, g_ref, xs_zero_ref, xs_ref, hbuf, sem, *, tm):
    del xs_zero_ref
    i = pl.program_id(0)
    slot = lax.rem(i, 2)

    @pl.when(i >= 2)
    def _():
        _row_copies_done(xs_ref, sem.at[slot], 2 * tm)

    hbuf[slot] = _rms(x_ref[...], g_ref[...])

    def issue(a, carry):
        tok = lax.shift_right_logical(a, 1)
        pltpu.make_async_copy(hbuf.at[slot, pl.ds(tok, 1), :],
                              xs_ref.at[pl.ds(pos_ref[0, a], 1), :],
                              sem.at[slot]).start()
        return carry

    lax.fori_loop(0, 2 * tm, issue, 0, unroll=8)

    @pl.when(i == pl.num_programs(0) - 1)
    def _():
        _row_copies_done(xs_ref, sem.at[slot], 2 * tm)

        @pl.when(i >= 1)
        def _():
            _row_copies_done(xs_ref, sem.at[1 - slot], 2 * tm)


def _dispatch(x, g, pos, xs_zero, *, tm):
    n = x.shape[0]
    return pl.pallas_call(
        functools.partial(_dispatch_kernel, tm=tm),
        grid=(n // tm,),
        in_specs=[
            pl.BlockSpec((None, 1, 2 * tm), lambda i: (i, 0, 0), memory_space=pltpu.SMEM),
            pl.BlockSpec((tm, D_MODEL), lambda i: (i, 0)),
            pl.BlockSpec((1, D_MODEL), lambda i: (0, 0)),
            pl.BlockSpec(memory_space=pl.ANY),
        ],
        out_specs=pl.BlockSpec(memory_space=pl.ANY),
        out_shape=jax.ShapeDtypeStruct(xs_zero.shape, F32),
        scratch_shapes=[
            pltpu.VMEM((2, tm, D_MODEL), F32),
            pltpu.SemaphoreType.DMA((2,)),
        ],
        input_output_aliases={3: 0},
        compiler_params=_params("arbitrary"),
        name="moe_dispatch",
    )(pos.reshape(n // tm, 1, 2 * tm), x, g, xs_zero)


def _expert_kernel(te_ref, tv_ref, x_ref, wg_ref, wu_ref, wd_ref, y_ref):
    t = pl.program_id(0)

    @pl.when(tv_ref[t] != 0)
    def _():
        h = x_ref[...].astype(BF16)
        acc = None
        for c0 in range(0, D_FF, FF_CHUNK):
            y = _swiglu(h, wg_ref[:, c0:c0 + FF_CHUNK], wu_ref[:, c0:c0 + FF_CHUNK],
                        wd_ref[c0:c0 + FF_CHUNK, :])
            acc = y if acc is None else acc + y
        y_ref[...] = acc

    @pl.when(tv_ref[t] == 0)
    def _():
        y_ref[...] = jnp.zeros_like(y_ref)


def _experts(xs, tile_expert, tile_valid, wg, wu, wd, *, tr):
    rows = xs.shape[0]
    wmap = lambda t, te, tv: (te[t], 0, 0)
    grid_spec = pltpu.PrefetchScalarGridSpec(
        num_scalar_prefetch=2,
        grid=(rows // tr,),
        in_specs=[
            pl.BlockSpec((tr, D_MODEL), lambda t, te, tv: (t, 0)),
            pl.BlockSpec((None, D_MODEL, D_FF), wmap),
            pl.BlockSpec((None, D_MODEL, D_FF), wmap),
            pl.BlockSpec((None, D_FF, D_MODEL), wmap),
        ],
        out_specs=pl.BlockSpec((tr, D_MODEL), lambda t, te, tv: (t, 0)),
    )
    return pl.pallas_call(
        _expert_kernel,
        grid_spec=grid_spec,
        out_shape=jax.ShapeDtypeStruct((rows, D_MODEL), F32),
        compiler_params=_params("arbitrary"),
        name="moe_experts",
    )(tile_expert, tile_valid, xs, wg, wu, wd)


def _combine_kernel(pos_ref, posn_ref, x_ref, rt_ref, ys_ref, o_ref, ybuf, sem, *, tm):
    i = pl.program_id(0)
    last = pl.num_programs(0) - 1
    slot = lax.rem(i, 2)

    def gather(p_ref, s):
        def issue(a, carry):
            tok = lax.shift_right_logical(a, 1)
            k = lax.bitwise_and(a, 1)
            pltpu.make_async_copy(ys_ref.at[pl.ds(p_ref[0, a], 1), :],
                                  ybuf.at[s, k, pl.ds(tok, 1), :],
                                  sem.at[s]).start()
            return carry
        lax.fori_loop(0, 2 * tm, issue, 0, unroll=8)

    @pl.when(i == 0)
    def _():
        gather(pos_ref, slot)

    @pl.when(i < last)
    def _():
        gather(posn_ref, 1 - slot)

    _row_copies_done(ys_ref, sem.at[slot], 2 * tm)
    rt = rt_ref[...]
    o_ref[...] = x_ref[...] + (rt[:, 2:3] * ybuf[slot, 0] + rt[:, 3:4] * ybuf[slot, 1])


def _combine(x, rt, pos, ys, *, tm):
    n = x.shape[0]
    nt = n // tm
    pos3 = pos.reshape(nt, 1, 2 * tm)
    row = lambda i: (i, 0)
    return pl.pallas_call(
        functools.partial(_combine_kernel, tm=tm),
        grid=(nt,),
        in_specs=[
            pl.BlockSpec((None, 1, 2 * tm), lambda i: (i, 0, 0), memory_space=pltpu.SMEM),
            pl.BlockSpec((None, 1, 2 * tm), lambda i: (jnp.minimum(i + 1, nt - 1), 0, 0),
                         memory_space=pltpu.SMEM),
            pl.BlockSpec((tm, D_MODEL), row),
            pl.BlockSpec((tm, LANES), row),
            pl.BlockSpec(memory_space=pl.ANY),
        ],
        out_specs=pl.BlockSpec((tm, D_MODEL), row),
        out_shape=jax.ShapeDtypeStruct((n, D_MODEL), F32),
        scratch_shapes=[
            pltpu.VMEM((2, 2, tm, D_MODEL), F32),
            pltpu.SemaphoreType.DMA((2,)),
        ],
        compiler_params=_params("arbitrary"),
        name="moe_combine",
    )(pos3, pos3, x, rt, ys)


def _moe(x, g, rhi, rlo, wg, wu, wd):
    n = x.shape[0]
    tm, tr = MOE_TOK_TILE, MOE_ROW_TILE
    rt = _route(x, g, rhi, rlo, tm=tm)
    expert = rt[:, :2].astype(jnp.int32).reshape(2 * n)
    onehot = (expert[:, None] == jnp.arange(N_EXPERTS, dtype=jnp.int32)[None, :]).astype(jnp.int32)
    csum = jnp.cumsum(onehot, axis=0)
    counts = csum[-1]
    group = ((counts + tr - 1) // tr) * tr
    group_end = jnp.cumsum(group)
    pos = jnp.sum(onehot * (csum - 1 + (group_end - group)[None, :]), axis=1).astype(jnp.int32)
    n_rows = 2 * n + N_EXPERTS * tr
    tile_start = jnp.arange(n_rows // tr, dtype=jnp.int32) * tr
    tile_expert = jnp.minimum(jnp.sum(tile_start[:, None] >= group_end[None, :], axis=1),
                              N_EXPERTS - 1).astype(jnp.int32)
    tile_valid = (tile_start < group_end[-1]).astype(jnp.int32)

    xs = _dispatch(x, g, pos, jnp.zeros((n_rows, D_MODEL), F32), tm=tm)
    ys = _experts(xs, tile_expert, tile_valid, wg, wu, wd, tr=tr)
    return _combine(x, rt, pos, ys, tm=tm)


def _head_selectors():
    sel = np.zeros((QK_WIDTH, LANES), np.float32)
    sel[np.arange(QK_WIDTH), np.arange(QK_WIDTH) // HEAD_DIM] = 1.0
    return jnp.asarray(sel, BF16), jnp.asarray(sel.T.copy(), BF16)


def _prompt_bias():
    slopes = _alibi_slopes(N_Q_HEADS)
    qi = np.arange(WINDOW)[:, None]
    kj = np.arange(2 * WINDOW)[None, :]
    dist = (qi + WINDOW - kj).astype(np.float32)
    valid = (dist >= 0) & (dist < WINDOW)
    out = np.empty((N_KV_HEADS, GQA_GROUP * WINDOW, 2 * WINDOW), np.float32)
    for g in range(N_KV_HEADS):
        for r in range(GQA_GROUP):
            b = np.where(valid, -(slopes[GQA_GROUP * g + r] * dist), -np.inf)
            out[g, r * WINDOW:(r + 1) * WINDOW] = b
    return jnp.asarray(out)


def _sample_bias(t_new, n_buf):
    slopes = _alibi_slopes(N_Q_HEADS)
    t = np.arange(t_new)[:, None]
    j = np.arange(SAMPLE_KEYS)[None, :]
    dist = (t + n_buf - j).astype(np.float32)
    valid = (dist >= 0) & (dist < WINDOW) & (j < n_buf + t_new)
    out = np.empty((N_KV_HEADS, SAMPLE_ROWS, SAMPLE_KEYS), np.float32)
    for g in range(N_KV_HEADS):
        for r in range(GQA_GROUP + 1):
            h = GQA_GROUP * g + min(r, GQA_GROUP - 1)
            out[g, r * t_new:(r + 1) * t_new] = np.where(valid, -(slopes[h] * dist), -np.inf)
    return jnp.asarray(out)


def _block_diag_pool(pool_w_l):
    bd = jnp.zeros((POOL_WIDTH, POOL_WIDTH), F32)
    for g in range(len(POOL_WINDOWS)):
        sl = slice(g * POOL_GROUP_DIM, (g + 1) * POOL_GROUP_DIM)
        bd = bd.at[sl, sl].set(pool_w_l[g])
    return bd.astype(BF16)


def kernel(x_prompt, x_sample, state_pool, state_win_k, state_win_v, norm_mix, w_in, q_norm, k_norm,
           attn_sinks, pool_w, pool_scale, w_out, norm_ffn, ffn_w_gate, ffn_w_up, ffn_w_down,
           moe_router, moe_w_gate, moe_w_up, moe_w_down):
    batch, seq, _ = x_prompt.shape
    n_seq, t_new, _ = x_sample.shape
    n_buf = state_win_k.shape[2]
    n_keep = min(WINDOW, seq)
    assert t_new == 8 and n_buf == WINDOW and SAMPLE_ROWS == (GQA_GROUP + 1) * t_new

    sel, selt = _head_selectors()
    bias_p = _prompt_bias()
    bias_s = _sample_bias(t_new, n_buf)

    xp = x_prompt.reshape(batch * seq, D_MODEL)
    xs = x_sample.reshape(n_seq * t_new, D_MODEL)
    pool_p, k_p, v_p, pool_s, k_s, v_s = [], [], [], [], [], []
    for l in range(DEPTH):
        g_mix = norm_mix[l].reshape(1, D_MODEL)
        g_ffn = norm_ffn[l].reshape(1, D_MODEL)
        w_in_b = w_in[l].astype(BF16)
        w_out_b = w_out[l].astype(BF16)
        gain = jnp.concatenate([jnp.tile(q_norm[l], N_Q_HEADS) * (HEAD_DIM ** -0.5),
                                jnp.tile(k_norm[l], N_KV_HEADS)]).reshape(1, QK_WIDTH)
        poolw_bd = _block_diag_pool(pool_w[l])
        pscale = pool_scale[l].reshape(1, POOL_WIDTH)
        sinks = attn_sinks[l].astype(F32)
        sink_rows = jnp.repeat(
            jnp.concatenate([sinks.reshape(N_KV_HEADS, GQA_GROUP), sinks.reshape(N_KV_HEADS, GQA_GROUP)[:, -1:]], axis=1),
            t_new, axis=1)
        sinkcol = jnp.broadcast_to(sink_rows[:, :, None], (N_KV_HEADS, SAMPLE_ROWS, LANES))

        i = l // 2
        if l % 2 == 0:
            wg = ffn_w_gate[i].astype(BF16)
            wu = ffn_w_up[i].astype(BF16)
            wd = ffn_w_down[i].astype(BF16)
            channel_mix = lambda x: _ffn(x, g_ffn, wg, wu, wd, tm=512, tf=1408)
        else:
            wg = moe_w_gate[i].astype(BF16)
            wu = moe_w_up[i].astype(BF16)
            wd = moe_w_down[i].astype(BF16)
            rpad = jnp.pad(moe_router[i], ((0, 0), (0, LANES - N_EXPERTS)))
            rhi, rlo = _split_bf16(rpad)
            channel_mix = lambda x: _moe(x, g_ffn, rhi, rlo, wg, wu, wd)

        u, q, k, v = _project(xp, g_mix, w_in_b, sel, selt, gain, tm=512, q_dtype=BF16)
        xp = _prompt_mix(xp, q, k, v, u, sinks, bias_p, poolw_bd, pscale, w_out_b,
                         batch=batch, seq=seq, tq=512)
        xp = channel_mix(xp)
        pool_p.append(u.reshape(batch, seq, POOL_WIDTH)[:, seq - POOL_BUF:])
        k_p.append(k.reshape(batch, seq, N_KV_HEADS, HEAD_DIM)[:, seq - n_keep:])
        v_p.append(v.reshape(batch, seq, N_KV_HEADS, HEAD_DIM)[:, seq - n_keep:])

        u, q, k, v = _project(xs, g_mix, w_in_b, sel, selt, gain, tm=512, q_dtype=F32)
        xs, ko, vo, po = _sample_mix(
            xs, q, k, v, u,
            state_win_k[l].reshape(n_seq, n_buf, KV_WIDTH),
            state_win_v[l].reshape(n_seq, n_buf, KV_WIDTH),
            state_pool[l], bias_s, sinkcol, poolw_bd, pscale, w_out_b,
            n_seq=n_seq, t_new=t_new, group=16)
        xs = channel_mix(xs)
        pool_s.append(po)
        k_s.append(ko.reshape(n_seq, n_buf, N_KV_HEADS, HEAD_DIM))
        v_s.append(vo.reshape(n_seq, n_buf, N_KV_HEADS, HEAD_DIM))

    return (xp.reshape(batch, seq, D_MODEL), xs.reshape(n_seq, t_new, D_MODEL),
            jnp.stack(pool_p), jnp.stack(k_p), jnp.stack(v_p),
            jnp.stack(pool_s), jnp.stack(k_s), jnp.stack(v_s))
```

```python
import functools
import math

import numpy as np
import jax
import jax.numpy as jnp
from jax import lax
from jax.experimental import pallas as pl
from jax.experimental.pallas import tpu as pltpu

F32 = jnp.float32
BF16 = jnp.bfloat16

D_MODEL = 1024
DEPTH = 2
POOL_WIDTH = 256
POOL_GROUP_DIM = 64
POOL_WINDOWS = (2, 4, 8, 16)
POOL_BUF = 15
HEAD_DIM = 64
N_Q_HEADS = 12
N_KV_HEADS = 4
GQA_GROUP = 3
ATTN_WIDTH = 768
KV_WIDTH = 256
QK_WIDTH = ATTN_WIDTH + KV_WIDTH
IN_WIDTH = 1536
WINDOW = 128
D_FF = 2816
N_EXPERTS = 8
EPS = 1e-6
NEG_INF = float("-inf")

LANES = 128
VMEM_LIMIT = 56 * 1024 * 1024


def _alibi_slopes(n):
    def pow2_slopes(m):
        start = 2.0 ** (-8.0 / m)
        return [start ** (i + 1) for i in range(m)]
    if float(math.log2(n)).is_integer():
        s = pow2_slopes(n)
    else:
        c = 2 ** int(math.floor(math.log2(n)))
        s = pow2_slopes(c) + pow2_slopes(2 * c)[0::2][: n - c]
    return np.array(s, dtype=np.float32)


def _split_bf16(x):
    hi = x.astype(BF16)
    lo = (x - hi.astype(F32)).astype(BF16)
    return hi, lo


def _params(*sem):
    return pltpu.CompilerParams(dimension_semantics=sem, vmem_limit_bytes=VMEM_LIMIT)


def _proj_kernel(x_ref, g_ref, w_ref, sel_ref, selt_ref, gain_ref,
                 u_ref, q_ref, k_ref, v_ref):
    xf = x_ref[...]
    ms = jnp.mean(xf * xf, axis=-1, keepdims=True)
    h = (xf * lax.rsqrt(ms + EPS) * g_ref[...]).astype(BF16)
    proj = jnp.dot(h, w_ref[...], preferred_element_type=F32)
    u_ref[...] = proj[:, :POOL_WIDTH]
    v_ref[...] = proj[:, POOL_WIDTH + QK_WIDTH:]
    qk = proj[:, POOL_WIDTH:POOL_WIDTH + QK_WIDTH]
    hi, lo = _split_bf16(qk * qk)
    ss = (jnp.dot(hi, sel_ref[...], preferred_element_type=F32)
          + jnp.dot(lo, sel_ref[...], preferred_element_type=F32))
    scale = lax.rsqrt(ss * (1.0 / HEAD_DIM) + EPS)
    shi, slo = _split_bf16(scale)
    full = (jnp.dot(shi, selt_ref[...], preferred_element_type=F32)
            + jnp.dot(slo, selt_ref[...], preferred_element_type=F32))
    qkn = qk * full * gain_ref[...]
    q_ref[...] = qkn[:, :ATTN_WIDTH].astype(q_ref.dtype)
    k_ref[...] = qkn[:, ATTN_WIDTH:]


def _project(x, g, w_in_b, sel, selt, gain, *, tm, q_dtype):
    n = x.shape[0]
    const = lambda i: (0, 0)
    row = lambda i: (i, 0)
    return pl.pallas_call(
        _proj_kernel,
        grid=(n // tm,),
        in_specs=[
            pl.BlockSpec((tm, D_MODEL), row),
            pl.BlockSpec((1, D_MODEL), const),
            pl.BlockSpec((D_MODEL, IN_WIDTH), const),
            pl.BlockSpec((QK_WIDTH, LANES), const),
            pl.BlockSpec((LANES, QK_WIDTH), const),
            pl.BlockSpec((1, QK_WIDTH), const),
        ],
        out_specs=[
            pl.BlockSpec((tm, POOL_WIDTH), row),
            pl.BlockSpec((tm, ATTN_WIDTH), row),
            pl.BlockSpec((tm, KV_WIDTH), row),
            pl.BlockSpec((tm, KV_WIDTH), row),
        ],
        out_shape=[
            jax.ShapeDtypeStruct((n, POOL_WIDTH), F32),
            jax.ShapeDtypeStruct((n, ATTN_WIDTH), q_dtype),
            jax.ShapeDtypeStruct((n, KV_WIDTH), F32),
            jax.ShapeDtypeStruct((n, KV_WIDTH), F32),
        ],
        compiler_params=_params("arbitrary"),
        name="norm_proj",
    )(x, g, w_in_b, sel, selt, gain)


def _pool_window_sums(load_shifted, lane_axis=1):
    x0 = load_shifted(0)
    acc = x0 + load_shifted(1)
    s2 = acc
    for d in range(2, 4):
        acc = acc + load_shifted(d)
    s4 = acc
    for d in range(4, 8):
        acc = acc + load_shifted(d)
    s8 = acc
    for d in range(8, 16):
        acc = acc + load_shifted(d)
    s16 = acc
    lane = lax.broadcasted_iota(jnp.int32, x0.shape, lane_axis)
    win = jnp.where(lane < 64, s2, jnp.where(lane < 128, s4, jnp.where(lane < 192, s8, s16)))
    return x0, win, lane


def _pool_window_len(lane):
    return jnp.where(lane < 64, 2, jnp.where(lane < 128, 4, jnp.where(lane < 192, 8, 16)))


def _prompt_mix_kernel(sink_ref, x_ref, q_ref, kc_ref, kp_ref, vc_ref, vp_ref, uc_ref, up_ref,
                       bias_ref, poolw_ref, pscale_ref, wout_ref,
                       o_ref, kk, vv, uu, mix, *, tq):
    i = pl.program_id(1)
    first = i == 0
    blk = WINDOW
    kk[0:blk] = kp_ref[...].astype(BF16)
    kk[blk:] = kc_ref[...].astype(BF16)
    vv[0:blk] = vp_ref[...].astype(BF16)
    vv[blk:] = vc_ref[...].astype(BF16)
    uu[0:16] = jnp.where(first, 0.0, up_ref[...])
    uu[16:] = uc_ref[...]

    col = lax.broadcasted_iota(jnp.int32, (1, 2 * blk), 1)
    no_prev = jnp.where(jnp.logical_and(first, col < blk), NEG_INF, 0.0)

    for j in range(tq // blk):
        r0 = j * blk
        x0, win, lane = _pool_window_sums(lambda d: uu[16 + r0 - d:16 + r0 - d + blk, :])
        pos = i * tq + r0 + lax.broadcasted_iota(jnp.int32, x0.shape, 0)
        cnt = jnp.minimum(_pool_window_len(lane), pos + 1).astype(F32)
        pooled = (win / cnt - x0).astype(BF16)
        pm = jnp.dot(pooled, poolw_ref[...], preferred_element_type=F32) * pscale_ref[...]
        mix[r0:r0 + blk, 0:POOL_WIDTH] = pm.astype(BF16)
        for g in range(N_KV_HEADS):
            kh = kk[r0:r0 + 2 * blk, HEAD_DIM * g:HEAD_DIM * (g + 1)]
            va = jnp.concatenate([vv[r0:r0 + 2 * blk, HEAD_DIM * g:HEAD_DIM * (g + 1)],
                                  jnp.ones((2 * blk, HEAD_DIM), BF16)], axis=1)
            for r in range(GQA_GROUP):
                h = GQA_GROUP * g + r
                qh = q_ref[r0:r0 + blk, HEAD_DIM * h:HEAD_DIM * (h + 1)]
                s = lax.dot_general(qh, kh, (((1,), (1,)), ((), ())),
                                    preferred_element_type=F32)
                s = s + bias_ref[g, r * blk:(r + 1) * blk, :]
                if j == 0:
                    s = s + no_prev
                sink = sink_ref[h]
                m = jnp.maximum(jnp.max(s, axis=-1, keepdims=True), sink)
                p = jnp.exp(s - m).astype(BF16)
                oa = jnp.dot(p, va, preferred_element_type=F32)
                den = oa[:, HEAD_DIM:HEAD_DIM + 1] + jnp.exp(sink - m)
                c0 = POOL_WIDTH + HEAD_DIM * h
                mix[r0:r0 + blk, c0:c0 + HEAD_DIM] = (oa[:, :HEAD_DIM] / den).astype(BF16)
        o_ref[r0:r0 + blk, :] = x_ref[r0:r0 + blk, :] + jnp.dot(
            mix[r0:r0 + blk, :], wout_ref[...], preferred_element_type=F32)


def _prompt_mix(x, q, k, v, u, sinks, bias, poolw_bd, pscale, wout_b, *, batch, seq, tq):
    nb = seq // tq
    x3 = x.reshape(batch, seq, D_MODEL)
    q3 = q.reshape(batch, seq, ATTN_WIDTH)
    k3 = k.reshape(batch, seq, KV_WIDTH)
    v3 = v.reshape(batch, seq, KV_WIDTH)
    u3 = u.reshape(batch, seq, POOL_WIDTH)
    cur = lambda b, i: (b, i, 0)
    prev_kv = lambda b, i: (b, jnp.maximum(i * (tq // WINDOW) - 1, 0), 0)
    prev_u = lambda b, i: (b, jnp.maximum(i * (tq // 16) - 1, 0), 0)
    const2 = lambda b, i: (0, 0)
    const3 = lambda b, i: (0, 0, 0)
    out = pl.pallas_call(
        functools.partial(_prompt_mix_kernel, tq=tq),
        grid=(batch, nb),
        in_specs=[
            pl.BlockSpec(memory_space=pltpu.SMEM),
            pl.BlockSpec((None, tq, D_MODEL), cur),
            pl.BlockSpec((None, tq, ATTN_WIDTH), cur),
            pl.BlockSpec((None, tq, KV_WIDTH), cur),
            pl.BlockSpec((None, WINDOW, KV_WIDTH), prev_kv),
            pl.BlockSpec((None, tq, KV_WIDTH), cur),
            pl.BlockSpec((None, WINDOW, KV_WIDTH), prev_kv),
            pl.BlockSpec((None, tq, POOL_WIDTH), cur),
            pl.BlockSpec((None, 16, POOL_WIDTH), prev_u),
            pl.BlockSpec((N_KV_HEADS, GQA_GROUP * WINDOW, 2 * WINDOW), const3),
            pl.BlockSpec((POOL_WIDTH, POOL_WIDTH), const2),
            pl.BlockSpec((1, POOL_WIDTH), const2),
            pl.BlockSpec((D_MODEL, D_MODEL), const2),
        ],
        out_specs=pl.BlockSpec((None, tq, D_MODEL), cur),
        out_shape=jax.ShapeDtypeStruct((batch, seq, D_MODEL), F32),
        scratch_shapes=[
            pltpu.VMEM((tq + WINDOW, KV_WIDTH), BF16),
            pltpu.VMEM((tq + WINDOW, KV_WIDTH), BF16),
            pltpu.VMEM((tq + 16, POOL_WIDTH), F32),
            pltpu.VMEM((tq, D_MODEL), BF16),
        ],
        compiler_params=_params("arbitrary", "arbitrary"),
        name="prompt_mix",
    )(sinks, x3, q3, k3, k3, v3, v3, u3, u3, bias, poolw_bd, pscale, wout_b)
    return out.reshape(batch * seq, D_MODEL)


SAMPLE_ROWS = 32
SAMPLE_KEYS = 256


def _sample_mix_kernel(x_ref, q_ref, kn_ref, vn_ref, un_ref, kb_ref, vb_ref, pb_ref,
                       bias_ref, sinkcol_ref, poolw_ref, pscale_ref, wout_ref,
                       o_ref, ko_ref, vo_ref, po_ref,
                       kx, vx, px, mix, *, group, t_new, n_buf):
    rows = group * t_new
    ko_ref[:, 0:n_buf - t_new, :] = kb_ref[:, t_new:n_buf, :]
    ko_ref[:, n_buf - t_new:n_buf, :] = kn_ref[...]
    vo_ref[:, 0:n_buf - t_new, :] = vb_ref[:, t_new:n_buf, :]
    vo_ref[:, n_buf - t_new:n_buf, :] = vn_ref[...]
    pad8 = jnp.zeros((group, 16 - t_new, KV_WIDTH), F32)
    tail = jnp.zeros((group, SAMPLE_KEYS - n_buf - 16, KV_WIDTH), BF16)
    kx[:, 0:n_buf, :] = kb_ref[...].astype(BF16)
    kx[:, n_buf:n_buf + 16, :] = jnp.concatenate([kn_ref[...], pad8], axis=1).astype(BF16)
    kx[:, n_buf + 16:, :] = tail
    vx[:, 0:n_buf, :] = vb_ref[...].astype(BF16)
    vx[:, n_buf:n_buf + 16, :] = jnp.concatenate([vn_ref[...], pad8], axis=1).astype(BF16)
    vx[:, n_buf + 16:, :] = tail

    px[:, 1:16, :] = pb_ref[...]
    px[:, 16:16 + t_new, :] = un_ref[...]
    po_ref[...] = px[:, 16 + t_new - POOL_BUF:16 + t_new, :]
    x0, win, lane = _pool_window_sums(lambda d: px[:, 16 - d:16 - d + t_new, :], lane_axis=2)
    cnt = _pool_window_len(lane).astype(F32)
    pooled = (win / cnt - x0).reshape(rows, POOL_WIDTH).astype(BF16)
    pm = jnp.dot(pooled, poolw_ref[...], preferred_element_type=F32) * pscale_ref[...]
    mix[:, :, 0:POOL_WIDTH] = pm.reshape(group, t_new, POOL_WIDTH)

    q3 = q_ref[...]
    for g in range(N_KV_HEADS):
        heads = [q3[:, :, HEAD_DIM * (GQA_GROUP * g + r):HEAD_DIM * (GQA_GROUP * g + r + 1)]
                 for r in range(GQA_GROUP)]
        qs = jnp.concatenate(heads + [heads[-1]], axis=1).astype(BF16)
        kh = kx[:, :, HEAD_DIM * g:HEAD_DIM * (g + 1)]
        vh = vx[:, :, HEAD_DIM * g:HEAD_DIM * (g + 1)]
        s = jnp.einsum("gqd,gkd->gqk", qs, kh, preferred_element_type=F32) + bias_ref[g][None]
        sink = sinkcol_ref[g][:, 0:1][None]
        m = jnp.maximum(jnp.max(s, axis=-1, keepdims=True), sink)
        p = jnp.exp(s - m)
        den = jnp.sum(p, axis=-1, keepdims=True) + jnp.exp(sink - m)
        o = jnp.einsum("gqk,gkd->gqd", p.astype(BF16), vh, preferred_element_type=F32) / den
        for r in range(GQA_GROUP):
            c0 = POOL_WIDTH + HEAD_DIM * (GQA_GROUP * g + r)
            mix[:, :, c0:c0 + HEAD_DIM] = o[:, r * t_new:(r + 1) * t_new, :]

    mixb = mix[...].reshape(rows, D_MODEL).astype(BF16)
    o_ref[...] = x_ref[...] + jnp.dot(mixb, wout_ref[...], preferred_element_type=F32)


def _sample_mix(x, q, k, v, u, kbuf, vbuf, pbuf, bias, sinkcol, poolw_bd, pscale, wout_b,
                *, n_seq, t_new, group):
    n_buf = kbuf.shape[1]
    q3 = q.reshape(n_seq, t_new, ATTN_WIDTH)
    k3 = k.reshape(n_seq, t_new, KV_WIDTH)
    v3 = v.reshape(n_seq, t_new, KV_WIDTH)
    u3 = u.reshape(n_seq, t_new, POOL_WIDTH)
    rows = group * t_new
    seq3 = lambda i: (i, 0, 0)
    row2 = lambda i: (i, 0)
    const2 = lambda i: (0, 0)
    const3 = lambda i: (0, 0, 0)
    return pl.pallas_call(
        functools.partial(_sample_mix_kernel, group=group, t_new=t_new, n_buf=n_buf),
        grid=(n_seq // group,),
        in_specs=[
            pl.BlockSpec((rows, D_MODEL), row2),
            pl.BlockSpec((group, t_new, ATTN_WIDTH), seq3),
            pl.BlockSpec((group, t_new, KV_WIDTH), seq3),
            pl.BlockSpec((group, t_new, KV_WIDTH), seq3),
            pl.BlockSpec((group, t_new, POOL_WIDTH), seq3),
            pl.BlockSpec((group, n_buf, KV_WIDTH), seq3),
            pl.BlockSpec((group, n_buf, KV_WIDTH), seq3),
            pl.BlockSpec((group, POOL_BUF, POOL_WIDTH), seq3),
            pl.BlockSpec((N_KV_HEADS, SAMPLE_ROWS, SAMPLE_KEYS), const3),
            pl.BlockSpec((N_KV_HEADS, SAMPLE_ROWS, LANES), const3),
            pl.BlockSpec((POOL_WIDTH, POOL_WIDTH), const2),
            pl.BlockSpec((1, POOL_WIDTH), const2),
            pl.BlockSpec((D_MODEL, D_MODEL), const2),
        ],
        out_specs=[
            pl.BlockSpec((rows, D_MODEL), row2),
            pl.BlockSpec((group, n_buf, KV_WIDTH), seq3),
            pl.BlockSpec((group, n_buf, KV_WIDTH), seq3),
            pl.BlockSpec((group, POOL_BUF, POOL_WIDTH), seq3),
        ],
        out_shape=[
            jax.ShapeDtypeStruct((n_seq * t_new, D_MODEL), F32),
            jax.ShapeDtypeStruct((n_seq, n_buf, KV_WIDTH), F32),
            jax.ShapeDtypeStruct((n_seq, n_buf, KV_WIDTH), F32),
            jax.ShapeDtypeStruct((n_seq, POOL_BUF, POOL_WIDTH), F32),
        ],
        scratch_shapes=[
            pltpu.VMEM((group, SAMPLE_KEYS, KV_WIDTH), BF16),
            pltpu.VMEM((group, SAMPLE_KEYS, KV_WIDTH), BF16),
            pltpu.VMEM((group, 16 + t_new, POOL_WIDTH), F32),
            pltpu.VMEM((group, t_new, D_MODEL), F32),
        ],
        compiler_params=_params("arbitrary"),
        name="sample_mix",
    )(x, q3, k3, v3, u3, kbuf, vbuf, pbuf, bias, sinkcol, poolw_bd, pscale, wout_b)


def _rms(xf, g):
    ms = jnp.mean(xf * xf, axis=-1, keepdims=True)
    return xf * lax.rsqrt(ms + EPS) * g


def _swiglu(h, wg, wu, wd):
    gate = jnp.dot(h, wg, preferred_element_type=F32)
    up = jnp.dot(h, wu, preferred_element_type=F32)
    act = (gate * jax.nn.sigmoid(gate) * up).astype(BF16)
    return jnp.dot(act, wd, preferred_element_type=F32)


def _ffn_kernel(x_ref, g_ref, wg_ref, wu_ref, wd_ref, o_ref, hb, acc):
    f = pl.program_id(1)

    @pl.when(f == 0)
    def _():
        hb[...] = _rms(x_ref[...], g_ref[...]).astype(BF16)
        acc[...] = jnp.zeros_like(acc)

    acc[...] += _swiglu(hb[...], wg_ref[...], wu_ref[...], wd_ref[...])

    @pl.when(f == pl.num_programs(1) - 1)
    def _():
        o_ref[...] = x_ref[...] + acc[...]


def _ffn(x, g, wg, wu, wd, *, tm, tf):
    n = x.shape[0]
    dff = wg.shape[1]
    row = lambda i, f: (i, 0)
    const = lambda i, f: (0, 0)
    return pl.pallas_call(
        _ffn_kernel,
        grid=(n // tm, dff // tf),
        in_specs=[
            pl.BlockSpec((tm, D_MODEL), row),
            pl.BlockSpec((1, D_MODEL), const),
            pl.BlockSpec((D_MODEL, tf), lambda i, f: (0, f)),
            pl.BlockSpec((D_MODEL, tf), lambda i, f: (0, f)),
            pl.BlockSpec((tf, D_MODEL), lambda i, f: (f, 0)),
        ],
        out_specs=pl.BlockSpec((tm, D_MODEL), row),
        out_shape=jax.ShapeDtypeStruct((n, D_MODEL), F32),
        scratch_shapes=[
            pltpu.VMEM((tm, D_MODEL), BF16),
            pltpu.VMEM((tm, D_MODEL), F32),
        ],
        compiler_params=_params("arbitrary", "arbitrary"),
        name="dense_ffn",
    )(x, g, wg, wu, wd)


MOE_TOK_TILE = 256
MOE_ROW_TILE = 256
FF_CHUNK = 1408


def _route_kernel(x_ref, g_ref, rhi_ref, rlo_ref, rt_ref):
    h = _rms(x_ref[...], g_ref[...])
    hi, lo = _split_bf16(h)
    lg = (jnp.dot(hi, rhi_ref[...], preferred_element_type=F32)
          + jnp.dot(lo, rhi_ref[...], preferred_element_type=F32)
          + jnp.dot(hi, rlo_ref[...], preferred_element_type=F32))
    lane = lax.broadcasted_iota(jnp.int32, lg.shape, 1)
    lg = jnp.where(lane < N_EXPERTS, lg, NEG_INF)
    m1 = jnp.max(lg, axis=-1, keepdims=True)
    i1 = jnp.min(jnp.where(lg == m1, lane, LANES), axis=-1, keepdims=True)
    lg2 = jnp.where(lane == i1, NEG_INF, lg)
    m2 = jnp.max(lg2, axis=-1, keepdims=True)
    i2 = jnp.min(jnp.where(lg2 == m2, lane, LANES), axis=-1, keepdims=True)
    e2 = jnp.exp(m2 - m1)
    den = 1.0 + e2
    rt_ref[...] = jnp.where(lane == 0, i1.astype(F32),
                            jnp.where(lane == 1, i2.astype(F32),
                                      jnp.where(lane == 2, 1.0 / den,
                                                jnp.where(lane == 3, e2 / den, 0.0))))


def _route(x, g, rhi, rlo, *, tm):
    n = x.shape[0]
    row = lambda i: (i, 0)
    const = lambda i: (0, 0)
    return pl.pallas_call(
        _route_kernel,
        grid=(n // tm,),
        in_specs=[
            pl.BlockSpec((tm, D_MODEL), row),
            pl.BlockSpec((1, D_MODEL), const),
            pl.BlockSpec((D_MODEL, LANES), const),
            pl.BlockSpec((D_MODEL, LANES), const),
        ],
        out_specs=pl.BlockSpec((tm, LANES), row),
        out_shape=jax.ShapeDtypeStruct((n, LANES), F32),
        compiler_params=_params("arbitrary"),
        name="moe_route",
    )(x, g, rhi, rlo)


def _row_copies_done(hbm_ref, sem, n_rows):
    pltpu.make_async_copy(hbm_ref.at[pl.ds(0, n_rows), :], hbm_ref.at[pl.ds(0, n_rows), :], sem).wait()


def _dispatch_kernel(pos_ref, x_ref, g_ref, xs_zero_ref, xs_ref, hbuf, sem, *, tm):
    del xs_zero_ref
    i = pl.program_id(0)
    slot = lax.rem(i, 2)

    @pl.when(i >= 2)
    def _():
        _row_copies_done(xs_ref, sem.at[slot], 2 * tm)

    hbuf[slot] = _rms(x_ref[...], g_ref[...])

    def issue(a, carry):
        tok = lax.shift_right_logical(a, 1)
        pltpu.make_async_copy(hbuf.at[slot, pl.ds(tok, 1), :],
                              xs_ref.at[pl.ds(pos_ref[0, a], 1), :],
                              sem.at[slot]).start()
        return carry

    lax.fori_loop(0, 2 * tm, issue, 0, unroll=8)

    @pl.when(i == pl.num_programs(0) - 1)
    def _():
        _row_copies_done(xs_ref, sem.at[slot], 2 * tm)

        @pl.when(i >= 1)
        def _():
            _row_copies_done(xs_ref, sem.at[1 - slot], 2 * tm)


def _dispatch(x, g, pos, xs_zero, *, tm):
    n = x.shape[0]
    return pl.pallas_call(
        functools.partial(_dispatch_kernel, tm=tm),
        grid=(n // tm,),
        in_specs=[
            pl.BlockSpec((None, 1, 2 * tm), lambda i: (i, 0, 0), memory_space=pltpu.SMEM),
            pl.BlockSpec((tm, D_MODEL), lambda i: (i, 0)),
            pl.BlockSpec((1, D_MODEL), lambda i: (0, 0)),
            pl.BlockSpec(memory_space=pl.ANY),
        ],
        out_specs=pl.BlockSpec(memory_space=pl.ANY),
        out_shape=jax.ShapeDtypeStruct(xs_zero.shape, F32),
        scratch_shapes=[
            pltpu.VMEM((2, tm, D_MODEL), F32),
            pltpu.SemaphoreType.DMA((2,)),
        ],
        input_output_aliases={3: 0},
        compiler_params=_params("arbitrary"),
        name="moe_dispatch",
    )(pos.reshape(n // tm, 1, 2 * tm), x, g, xs_zero)


def _expert_kernel(te_ref, tv_ref, x_ref, wg_ref, wu_ref, wd_ref, y_ref):
    t = pl.program_id(0)

    @pl.when(tv_ref[t] != 0)
    def _():
        h = x_ref[...].astype(BF16)
        acc = None
        for c0 in range(0, D_FF, FF_CHUNK):
            y = _swiglu(h, wg_ref[:, c0:c0 + FF_CHUNK], wu_ref[:, c0:c0 + FF_CHUNK],
                        wd_ref[c0:c0 + FF_CHUNK, :])
            acc = y if acc is None else acc + y
        y_ref[...] = acc

    @pl.when(tv_ref[t] == 0)
    def _():
        y_ref[...] = jnp.zeros_like(y_ref)


def _experts(xs, tile_expert, tile_valid, wg, wu, wd, *, tr):
    rows = xs.shape[0]
    wmap = lambda t, te, tv: (te[t], 0, 0)
    grid_spec = pltpu.PrefetchScalarGridSpec(
        num_scalar_prefetch=2,
        grid=(rows // tr,),
        in_specs=[
            pl.BlockSpec((tr, D_MODEL), lambda t, te, tv: (t, 0)),
            pl.BlockSpec((None, D_MODEL, D_FF), wmap),
            pl.BlockSpec((None, D_MODEL, D_FF), wmap),
            pl.BlockSpec((None, D_FF, D_MODEL), wmap),
        ],
        out_specs=pl.BlockSpec((tr, D_MODEL), lambda t, te, tv: (t, 0)),
    )
    return pl.pallas_call(
        _expert_kernel,
        grid_spec=grid_spec,
        out_shape=jax.ShapeDtypeStruct((rows, D_MODEL), F32),
        compiler_params=_params("arbitrary"),
        name="moe_experts",
    )(tile_expert, tile_valid, xs, wg, wu, wd)


def _combine_kernel(pos_ref, posn_ref, x_ref, rt_ref, ys_ref, o_ref, ybuf, sem, *, tm):
    i = pl.program_id(0)
    last = pl.num_programs(0) - 1
    slot = lax.rem(i, 2)

    def gather(p_ref, s):
        def issue(a, carry):
            tok = lax.shift_right_logical(a, 1)
            k = lax.bitwise_and(a, 1)
            pltpu.make_async_copy(ys_ref.at[pl.ds(p_ref[0, a], 1), :],
                                  ybuf.at[s, k, pl.ds(tok, 1), :],
                                  sem.at[s]).start()
            return carry
        lax.fori_loop(0, 2 * tm, issue, 0, unroll=8)

    @pl.when(i == 0)
    def _():
        gather(pos_ref, slot)

    @pl.when(i < last)
    def _():
        gather(posn_ref, 1 - slot)

    _row_copies_done(ys_ref, sem.at[slot], 2 * tm)
    rt = rt_ref[...]
    o_ref[...] = x_ref[...] + (rt[:, 2:3] * ybuf[slot, 0] + rt[:, 3:4] * ybuf[slot, 1])


def _combine(x, rt, pos, ys, *, tm):
    n = x.shape[0]
    nt = n // tm
    pos3 = pos.reshape(nt, 1, 2 * tm)
    row = lambda i: (i, 0)
    return pl.pallas_call(
        functools.partial(_combine_kernel, tm=tm),
        grid=(nt,),
        in_specs=[
            pl.BlockSpec((None, 1, 2 * tm), lambda i: (i, 0, 0), memory_space=pltpu.SMEM),
            pl.BlockSpec((None, 1, 2 * tm), lambda i: (jnp.minimum(i + 1, nt - 1), 0, 0),
                         memory_space=pltpu.SMEM),
            pl.BlockSpec((tm, D_MODEL), row),
            pl.BlockSpec((tm, LANES), row),
            pl.BlockSpec(memory_space=pl.ANY),
        ],
        out_specs=pl.BlockSpec((tm, D_MODEL), row),
        out_shape=jax.ShapeDtypeStruct((n, D_MODEL), F32),
        scratch_shapes=[
            pltpu.VMEM((2, 2, tm, D_MODEL), F32),
            pltpu.SemaphoreType.DMA((2,)),
        ],
        compiler_params=_params("arbitrary"),
        name="moe_combine",
    )(pos3, pos3, x, rt, ys)


def _moe(x, g, rhi, rlo, wg, wu, wd):
    n = x.shape[0]
    tm, tr = MOE_TOK_TILE, MOE_ROW_TILE
    rt = _route(x, g, rhi, rlo, tm=tm)
    expert = rt[:, :2].astype(jnp.int32).reshape(2 * n)
    onehot = (expert[:, None] == jnp.arange(N_EXPERTS, dtype=jnp.int32)[None, :]).astype(jnp.int32)
    csum = jnp.cumsum(onehot, axis=0)
    counts = csum[-1]
    group = ((counts + tr - 1) // tr) * tr
    group_end = jnp.cumsum(group)
    pos = jnp.sum(onehot * (csum - 1 + (group_end - group)[None, :]), axis=1).astype(jnp.int32)
    n_rows = 2 * n + N_EXPERTS * tr
    tile_start = jnp.arange(n_rows // tr, dtype=jnp.int32) * tr
    tile_expert = jnp.minimum(jnp.sum(tile_start[:, None] >= group_end[None, :], axis=1),
                              N_EXPERTS - 1).astype(jnp.int32)
    tile_valid = (tile_start < group_end[-1]).astype(jnp.int32)

    xs = _dispatch(x, g, pos, jnp.zeros((n_rows, D_MODEL), F32), tm=tm)
    ys = _experts(xs, tile_expert, tile_valid, wg, wu, wd, tr=tr)
    return _combine(x, rt, pos, ys, tm=tm)


def _head_selectors():
    sel = np.zeros((QK_WIDTH, LANES), np.float32)
    sel[np.arange(QK_WIDTH), np.arange(QK_WIDTH) // HEAD_DIM] = 1.0
    return jnp.asarray(sel, BF16), jnp.asarray(sel.T.copy(), BF16)


def _prompt_bias():
    slopes = _alibi_slopes(N_Q_HEADS)
    qi = np.arange(WINDOW)[:, None]
    kj = np.arange(2 * WINDOW)[None, :]
    dist = (qi + WINDOW - kj).astype(np.float32)
    valid = (dist >= 0) & (dist < WINDOW)
    out = np.empty((N_KV_HEADS, GQA_GROUP * WINDOW, 2 * WINDOW), np.float32)
    for g in range(N_KV_HEADS):
        for r in range(GQA_GROUP):
            b = np.where(valid, -(slopes[GQA_GROUP * g + r] * dist), -np.inf)
            out[g, r * WINDOW:(r + 1) * WINDOW] = b
    return jnp.asarray(out)


def _sample_bias(t_new, n_buf):
    slopes = _alibi_slopes(N_Q_HEADS)
    t = np.arange(t_new)[:, None]
    j = np.arange(SAMPLE_KEYS)[None, :]
    dist = (t + n_buf - j).astype(np.float32)
    valid = (dist >= 0) & (dist < WINDOW) & (j < n_buf + t_new)
    out = np.empty((N_KV_HEADS, SAMPLE_ROWS, SAMPLE_KEYS), np.float32)
    for g in range(N_KV_HEADS):
        for r in range(GQA_GROUP + 1):
            h = GQA_GROUP * g + min(r, GQA_GROUP - 1)
            out[g, r * t_new:(r + 1) * t_new] = np.where(valid, -(slopes[h] * dist), -np.inf)
    return jnp.asarray(out)


def _block_diag_pool(pool_w_l):
    bd = jnp.zeros((POOL_WIDTH, POOL_WIDTH), F32)
    for g in range(len(POOL_WINDOWS)):
        sl = slice(g * POOL_GROUP_DIM, (g + 1) * POOL_GROUP_DIM)
        bd = bd.at[sl, sl].set(pool_w_l[g])
    return bd.astype(BF16)


def kernel(x_prompt, x_sample, state_pool, state_win_k, state_win_v, norm_mix, w_in, q_norm, k_norm,
           attn_sinks, pool_w, pool_scale, w_out, norm_ffn, ffn_w_gate, ffn_w_up, ffn_w_down,
           moe_router, moe_w_gate, moe_w_up, moe_w_down):
    batch, seq, _ = x_prompt.shape
    n_seq, t_new, _ = x_sample.shape
    n_buf = state_win_k.shape[2]
    n_keep = min(WINDOW, seq)
    assert t_new == 8 and n_buf == WINDOW and SAMPLE_ROWS == (GQA_GROUP + 1) * t_new

    sel, selt = _head_selectors()
    bias_p = _prompt_bias()
    bias_s = _sample_bias(t_new, n_buf)

    xp = x_prompt.reshape(batch * seq, D_MODEL)
    xs = x_sample.reshape(n_seq * t_new, D_MODEL)
    pool_p, k_p, v_p, pool_s, k_s, v_s = [], [], [], [], [], []
    for l in range(DEPTH):
        g_mix = norm_mix[l].reshape(1, D_MODEL)
        g_ffn = norm_ffn[l].reshape(1, D_MODEL)
        w_in_b = w_in[l].astype(BF16)
        w_out_b = w_out[l].astype(BF16)
        gain = jnp.concatenate([jnp.tile(q_norm[l], N_Q_HEADS) * (HEAD_DIM ** -0.5),
                                jnp.tile(k_norm[l], N_KV_HEADS)]).reshape(1, QK_WIDTH)
        poolw_bd = _block_diag_pool(pool_w[l])
        pscale = pool_scale[l].reshape(1, POOL_WIDTH)
        sinks = attn_sinks[l].astype(F32)
        sink_rows = jnp.repeat(
            jnp.concatenate([sinks.reshape(N_KV_HEADS, GQA_GROUP), sinks.reshape(N_KV_HEADS, GQA_GROUP)[:, -1:]], axis=1),
            t_new, axis=1)
        sinkcol = jnp.broadcast_to(sink_rows[:, :, None], (N_KV_HEADS, SAMPLE_ROWS, LANES))

        i = l // 2
        if l % 2 == 0:
            wg = ffn_w_gate[i].astype(BF16)
            wu = ffn_w_up[i].astype(BF16)
            wd = ffn_w_down[i].astype(BF16)
            channel_mix = lambda x: _ffn(x, g_ffn, wg, wu, wd, tm=512, tf=1408)
        else:
            wg = moe_w_gate[i].astype(BF16)
            wu = moe_w_up[i].astype(BF16)
            wd = moe_w_down[i].astype(BF16)
            rpad = jnp.pad(moe_router[i], ((0, 0), (0, LANES - N_EXPERTS)))
            rhi, rlo = _split_bf16(rpad)
            channel_mix = lambda x: _moe(x, g_ffn, rhi, rlo, wg, wu, wd)

        u, q, k, v = _project(xp, g_mix, w_in_b, sel, selt, gain, tm=512, q_dtype=BF16)
        xp = _prompt_mix(xp, q, k, v, u, sinks, bias_p, poolw_bd, pscale, w_out_b,
                         batch=batch, seq=seq, tq=512)
        xp = channel_mix(xp)
        pool_p.append(u.reshape(batch, seq, POOL_WIDTH)[:, seq - POOL_BUF:])
        k_p.append(k.reshape(batch, seq, N_KV_HEADS, HEAD_DIM)[:, seq - n_keep:])
        v_p.append(v.reshape(batch, seq, N_KV_HEADS, HEAD_DIM)[:, seq - n_keep:])

        u, q, k, v = _project(xs, g_mix, w_in_b, sel, selt, gain, tm=512, q_dtype=F32)
        xs, ko, vo, po = _sample_mix(
            xs, q, k, v, u,
            state_win_k[l].reshape(n_seq, n_buf, KV_WIDTH),
            state_win_v[l].reshape(n_seq, n_buf, KV_WIDTH),
            state_pool[l], bias_s, sinkcol, poolw_bd, pscale, w_out_b,
            n_seq=n_seq, t_new=t_new, group=16)
        xs = channel_mix(xs)
        pool_s.append(po)
        k_s.append(ko.reshape(n_seq, n_buf, N_KV_HEADS, HEAD_DIM))
        v_s.append(vo.reshape(n_seq, n_buf, N_KV_HEADS, HEAD_DIM))

    return (xp.reshape(batch, seq, D_MODEL), xs.reshape(n_seq, t_new, D_MODEL),
            jnp.stack(pool_p), jnp.stack(k_p), jnp.stack(v_p),
            jnp.stack(pool_s), jnp.stack(k_s), jnp.stack(v_s))
```

```python
import functools
import math

import numpy as np
import jax
import jax.numpy as jnp
from jax import lax
from jax.experimental import pallas as pl
from jax.experimental.pallas import tpu as pltpu

F32 = jnp.float32
BF16 = jnp.bfloat16

D_MODEL = 1024
DEPTH = 2
POOL_WIDTH = 256
POOL_GROUP_DIM = 64
POOL_WINDOWS = (2, 4, 8, 16)
POOL_BUF = 15
HEAD_DIM = 64
N_Q_HEADS = 12
N_KV_HEADS = 4
GQA_GROUP = 3
ATTN_WIDTH = 768
KV_WIDTH = 256
QK_WIDTH = ATTN_WIDTH + KV_WIDTH
IN_WIDTH = 1536
WINDOW = 128
D_FF = 2816
N_EXPERTS = 8
EPS = 1e-6
NEG_INF = float("-inf")

LANES = 128
VMEM_LIMIT = 56 * 1024 * 1024


def _alibi_slopes(n):
    def pow2_slopes(m):
        start = 2.0 ** (-8.0 / m)
        return [start ** (i + 1) for i in range(m)]
    if float(math.log2(n)).is_integer():
        s = pow2_slopes(n)
    else:
        c = 2 ** int(math.floor(math.log2(n)))
        s = pow2_slopes(c) + pow2_slopes(2 * c)[0::2][: n - c]
    return np.array(s, dtype=np.float32)


def _split_bf16(x):
    hi = x.astype(BF16)
    lo = (x - hi.astype(F32)).astype(BF16)
    return hi, lo


def _params(*sem):
    return pltpu.CompilerParams(dimension_semantics=sem, vmem_limit_bytes=VMEM_LIMIT)


def _proj_kernel(x_ref, g_ref, w_ref, sel_ref, selt_ref, gain_ref,
                 u_ref, q_ref, k_ref, v_ref):
    xf = x_ref[...]
    ms = jnp.mean(xf * xf, axis=-1, keepdims=True)
    h = (xf * lax.rsqrt(ms + EPS) * g_ref[...]).astype(BF16)
    proj = jnp.dot(h, w_ref[...], preferred_element_type=F32)
    u_ref[...] = proj[:, :POOL_WIDTH]
    v_ref[...] = proj[:, POOL_WIDTH + QK_WIDTH:]
    qk = proj[:, POOL_WIDTH:POOL_WIDTH + QK_WIDTH]
    hi, lo = _split_bf16(qk * qk)
    ss = (jnp.dot(hi, sel_ref[...], preferred_element_type=F32)
          + jnp.dot(lo, sel_ref[...], preferred_element_type=F32))
    scale = lax.rsqrt(ss * (1.0 / HEAD_DIM) + EPS)
    shi, slo = _split_bf16(scale)
    full = (jnp.dot(shi, selt_ref[...], preferred_element_type=F32)
            + jnp.dot(slo, selt_ref[...], preferred_element_type=F32))
    qkn = qk * full * gain_ref[...]
    q_ref[...] = qkn[:, :ATTN_WIDTH].astype(q_ref.dtype)
    k_ref[...] = qkn[:, ATTN_WIDTH:]


def _project(x, g, w_in_b, sel, selt, gain, *, tm, q_dtype):
    n = x.shape[0]
    const = lambda i: (0, 0)
    row = lambda i: (i, 0)
    return pl.pallas_call(
        _proj_kernel,
        grid=(n // tm,),
        in_specs=[
            pl.BlockSpec((tm, D_MODEL), row),
            pl.BlockSpec((1, D_MODEL), const),
            pl.BlockSpec((D_MODEL, IN_WIDTH), const),
            pl.BlockSpec((QK_WIDTH, LANES), const),
            pl.BlockSpec((LANES, QK_WIDTH), const),
            pl.BlockSpec((1, QK_WIDTH), const),
        ],
        out_specs=[
            pl.BlockSpec((tm, POOL_WIDTH), row),
            pl.BlockSpec((tm, ATTN_WIDTH), row),
            pl.BlockSpec((tm, KV_WIDTH), row),
            pl.BlockSpec((tm, KV_WIDTH), row),
        ],
        out_shape=[
            jax.ShapeDtypeStruct((n, POOL_WIDTH), F32),
            jax.ShapeDtypeStruct((n, ATTN_WIDTH), q_dtype),
            jax.ShapeDtypeStruct((n, KV_WIDTH), F32),
            jax.ShapeDtypeStruct((n, KV_WIDTH), F32),
        ],
        compiler_params=_params("arbitrary"),
        name="norm_proj",
    )(x, g, w_in_b, sel, selt, gain)


def _pool_window_sums(load_shifted, lane_axis=1):
    x0 = load_shifted(0)
    acc = x0 + load_shifted(1)
    s2 = acc
    for d in range(2, 4):
        acc = acc + load_shifted(d)
    s4 = acc
    for d in range(4, 8):
        acc = acc + load_shifted(d)
    s8 = acc
    for d in range(8, 16):
        acc = acc + load_shifted(d)
    s16 = acc
    lane = lax.broadcasted_iota(jnp.int32, x0.shape, lane_axis)
    win = jnp.where(lane < 64, s2, jnp.where(lane < 128, s4, jnp.where(lane < 192, s8, s16)))
    return x0, win, lane


def _pool_window_len(lane):
    return jnp.where(lane < 64, 2, jnp.where(lane < 128, 4, jnp.where(lane < 192, 8, 16)))


def _prompt_mix_kernel(sink_ref, x_ref, q_ref, kc_ref, kp_ref, vc_ref, vp_ref, uc_ref, up_ref,
                       bias_ref, poolw_ref, pscale_ref, wout_ref,
                       o_ref, kk, vv, uu, mix, *, tq):
    i = pl.program_id(1)
    first = i == 0
    blk = WINDOW
    kk[0:blk] = kp_ref[...].astype(BF16)
    kk[blk:] = kc_ref[...].astype(BF16)
    vv[0:blk] = vp_ref[...].astype(BF16)
    vv[blk:] = vc_ref[...].astype(BF16)
    uu[0:16] = jnp.where(first, 0.0, up_ref[...])
    uu[16:] = uc_ref[...]

    col = lax.broadcasted_iota(jnp.int32, (1, 2 * blk), 1)
    no_prev = jnp.where(jnp.logical_and(first, col < blk), NEG_INF, 0.0)

    for j in range(tq // blk):
        r0 = j * blk
        x0, win, lane = _pool_window_sums(lambda d: uu[16 + r0 - d:16 + r0 - d + blk, :])
        pos = i * tq + r0 + lax.broadcasted_iota(jnp.int32, x0.shape, 0)
        cnt = jnp.minimum(_pool_window_len(lane), pos + 1).astype(F32)
        pooled = (win / cnt - x0).astype(BF16)
        pm = jnp.dot(pooled, poolw_ref[...], preferred_element_type=F32) * pscale_ref[...]
        mix[r0:r0 + blk, 0:POOL_WIDTH] = pm.astype(BF16)
        for g in range(N_KV_HEADS):
            kh = kk[r0:r0 + 2 * blk, HEAD_DIM * g:HEAD_DIM * (g + 1)]
            va = jnp.concatenate([vv[r0:r0 + 2 * blk, HEAD_DIM * g:HEAD_DIM * (g + 1)],
                                  jnp.ones((2 * blk, HEAD_DIM), BF16)], axis=1)
            for r in range(GQA_GROUP):
                h = GQA_GROUP * g + r
                qh = q_ref[r0:r0 + blk, HEAD_DIM * h:HEAD_DIM * (h + 1)]
                s = lax.dot_general(qh, kh, (((1,), (1,)), ((), ())),
                                    preferred_element_type=F32)
                s = s + bias_ref[g, r * blk:(r + 1) * blk, :]
                if j == 0:
                    s = s + no_prev
                sink = sink_ref[h]
                m = jnp.maximum(jnp.max(s, axis=-1, keepdims=True), sink)
                p = jnp.exp(s - m).astype(BF16)
                oa = jnp.dot(p, va, preferred_element_type=F32)
                den = oa[:, HEAD_DIM:HEAD_DIM + 1] + jnp.exp(sink - m)
                c0 = POOL_WIDTH + HEAD_DIM * h
                mix[r0:r0 + blk, c0:c0 + HEAD_DIM] = (oa[:, :HEAD_DIM] / den).astype(BF16)
        o_ref[r0:r0 + blk, :] = x_ref[r0:r0 + blk, :] + jnp.dot(
            mix[r0:r0 + blk, :], wout_ref[...], preferred_element_type=F32)


def _prompt_mix(x, q, k, v, u, sinks, bias, poolw_bd, pscale, wout_b, *, batch, seq, tq):
    nb = seq // tq
    x3 = x.reshape(batch, seq, D_MODEL)
    q3 = q.reshape(batch, seq, ATTN_WIDTH)
    k3 = k.reshape(batch, seq, KV_WIDTH)
    v3 = v.reshape(batch, seq, KV_WIDTH)
    u3 = u.reshape(batch, seq, POOL_WIDTH)
    cur = lambda b, i: (b, i, 0)
    prev_kv = lambda b, i: (b, jnp.maximum(i * (tq // WINDOW) - 1, 0), 0)
    prev_u = lambda b, i: (b, jnp.maximum(i * (tq // 16) - 1, 0), 0)
    const2 = lambda b, i: (0, 0)
    const3 = lambda b, i: (0, 0, 0)
    out = pl.pallas_call(
        functools.partial(_prompt_mix_kernel, tq=tq),
        grid=(batch, nb),
        in_specs=[
            pl.BlockSpec(memory_space=pltpu.SMEM),
            pl.BlockSpec((None, tq, D_MODEL), cur),
            pl.BlockSpec((None, tq, ATTN_WIDTH), cur),
            pl.BlockSpec((None, tq, KV_WIDTH), cur),
            pl.BlockSpec((None, WINDOW, KV_WIDTH), prev_kv),
            pl.BlockSpec((None, tq, KV_WIDTH), cur),
            pl.BlockSpec((None, WINDOW, KV_WIDTH), prev_kv),
            pl.BlockSpec((None, tq, POOL_WIDTH), cur),
            pl.BlockSpec((None, 16, POOL_WIDTH), prev_u),
            pl.BlockSpec((N_KV_HEADS, GQA_GROUP * WINDOW, 2 * WINDOW), const3),
            pl.BlockSpec((POOL_WIDTH, POOL_WIDTH), const2),
            pl.BlockSpec((1, POOL_WIDTH), const2),
            pl.BlockSpec((D_MODEL, D_MODEL), const2),
        ],
        out_specs=pl.BlockSpec((None, tq, D_MODEL), cur),
        out_shape=jax.ShapeDtypeStruct((batch, seq, D_MODEL), F32),
        scratch_shapes=[
            pltpu.VMEM((tq + WINDOW, KV_WIDTH), BF16),
            pltpu.VMEM((tq + WINDOW, KV_WIDTH), BF16),
            pltpu.VMEM((tq + 16, POOL_WIDTH), F32),
            pltpu.VMEM((tq, D_MODEL), BF16),
        ],
        compiler_params=_params("arbitrary", "arbitrary"),
        name="prompt_mix",
    )(sinks, x3, q3, k3, k3, v3, v3, u3, u3, bias, poolw_bd, pscale, wout_b)
    return out.reshape(batch * seq, D_MODEL)


SAMPLE_ROWS = 32
SAMPLE_KEYS = 256


def _sample_mix_kernel(x_ref, q_ref, kn_ref, vn_ref, un_ref, kb_ref, vb_ref, pb_ref,
                       bias_ref, sinkcol_ref, poolw_ref, pscale_ref, wout_ref,
                       o_ref, ko_ref, vo_ref, po_ref,
                       kx, vx, px, mix, *, group, t_new, n_buf):
    rows = group * t_new
    ko_ref[:, 0:n_buf - t_new, :] = kb_ref[:, t_new:n_buf, :]
    ko_ref[:, n_buf - t_new:n_buf, :] = kn_ref[...]
    vo_ref[:, 0:n_buf - t_new, :] = vb_ref[:, t_new:n_buf, :]
    vo_ref[:, n_buf - t_new:n_buf, :] = vn_ref[...]
    pad8 = jnp.zeros((group, 16 - t_new, KV_WIDTH), F32)
    tail = jnp.zeros((group, SAMPLE_KEYS - n_buf - 16, KV_WIDTH), BF16)
    kx[:, 0:n_buf, :] = kb_ref[...].astype(BF16)
    kx[:, n_buf:n_buf + 16, :] = jnp.concatenate([kn_ref[...], pad8], axis=1).astype(BF16)
    kx[:, n_buf + 16:, :] = tail
    vx[:, 0:n_buf, :] = vb_ref[...].astype(BF16)
    vx[:, n_buf:n_buf + 16, :] = jnp.concatenate([vn_ref[...], pad8], axis=1).astype(BF16)
    vx[:, n_buf + 16:, :] = tail

    px[:, 1:16, :] = pb_ref[...]
    px[:, 16:16 + t_new, :] = un_ref[...]
    po_ref[...] = px[:, 16 + t_new - POOL_BUF:16 + t_new, :]
    x0, win, lane = _pool_window_sums(lambda d: px[:, 16 - d:16 - d + t_new, :], lane_axis=2)
    cnt = _pool_window_len(lane).astype(F32)
    pooled = (win / cnt - x0).reshape(rows, POOL_WIDTH).astype(BF16)
    pm = jnp.dot(pooled, poolw_ref[...], preferred_element_type=F32) * pscale_ref[...]
    mix[:, :, 0:POOL_WIDTH] = pm.reshape(group, t_new, POOL_WIDTH)

    q3 = q_ref[...]
    for g in range(N_KV_HEADS):
        heads = [q3[:, :, HEAD_DIM * (GQA_GROUP * g + r):HEAD_DIM * (GQA_GROUP * g + r + 1)]
                 for r in range(GQA_GROUP)]
        qs = jnp.concatenate(heads + [heads[-1]], axis=1).astype(BF16)
        kh = kx[:, :, HEAD_DIM * g:HEAD_DIM * (g + 1)]
        vh = vx[:, :, HEAD_DIM * g:HEAD_DIM * (g + 1)]
        s = jnp.einsum("gqd,gkd->gqk", qs, kh, preferred_element_type=F32) + bias_ref[g][None]
        sink = sinkcol_ref[g][:, 0:1][None]
        m = jnp.maximum(jnp.max(s, axis=-1, keepdims=True), sink)
        p = jnp.exp(s - m)
        den = jnp.sum(p, axis=-1, keepdims=True) + jnp.exp(sink - m)
        o = jnp.einsum("gqk,gkd->gqd", p.astype(BF16), vh, preferred_element_type=F32) / den
        for r in range(GQA_GROUP):
            c0 = POOL_WIDTH + HEAD_DIM * (GQA_GROUP * g + r)
            mix[:, :, c0:c0 + HEAD_DIM] = o[:, r * t_new:(r + 1) * t_new, :]

    mixb = mix[...].reshape(rows, D_MODEL).astype(BF16)
    o_ref[...] = x_ref[...] + jnp.dot(mixb, wout_ref[...], preferred_element_type=F32)


def _sample_mix(x, q, k, v, u, kbuf, vbuf, pbuf, bias, sinkcol, poolw_bd, pscale, wout_b,
                *, n_seq, t_new, group):
    n_buf = kbuf.shape[1]
    q3 = q.reshape(n_seq, t_new, ATTN_WIDTH)
    k3 = k.reshape(n_seq, t_new, KV_WIDTH)
    v3 = v.reshape(n_seq, t_new, KV_WIDTH)
    u3 = u.reshape(n_seq, t_new, POOL_WIDTH)
    rows = group * t_new
    seq3 = lambda i: (i, 0, 0)
    row2 = lambda i: (i, 0)
    const2 = lambda i: (0, 0)
    const3 = lambda i: (0, 0, 0)
    return pl.pallas_call(
        functools.partial(_sample_mix_kernel, group=group, t_new=t_new, n_buf=n_buf),
        grid=(n_seq // group,),
        in_specs=[
            pl.BlockSpec((rows, D_MODEL), row2),
            pl.BlockSpec((group, t_new, ATTN_WIDTH), seq3),
            pl.BlockSpec((group, t_new, KV_WIDTH), seq3),
            pl.BlockSpec((group, t_new, KV_WIDTH), seq3),
            pl.BlockSpec((group, t_new, POOL_WIDTH), seq3),
            pl.BlockSpec((group, n_buf, KV_WIDTH), seq3),
            pl.BlockSpec((group, n_buf, KV_WIDTH), seq3),
            pl.BlockSpec((group, POOL_BUF, POOL_WIDTH), seq3),
            pl.BlockSpec((N_KV_HEADS, SAMPLE_ROWS, SAMPLE_KEYS), const3),
            pl.BlockSpec((N_KV_HEADS, SAMPLE_ROWS, LANES), const3),
            pl.BlockSpec((POOL_WIDTH, POOL_WIDTH), const2),
            pl.BlockSpec((1, POOL_WIDTH), const2),
            pl.BlockSpec((D_MODEL, D_MODEL), const2),
        ],
        out_specs=[
            pl.BlockSpec((rows, D_MODEL), row2),
            pl.BlockSpec((group, n_buf, KV_WIDTH), seq3),
            pl.BlockSpec((group, n_buf, KV_WIDTH), seq3),
            pl.BlockSpec((group, POOL_BUF, POOL_WIDTH), seq3),
        ],
        out_shape=[
            jax.ShapeDtypeStruct((n_seq * t_new, D_MODEL), F32),
            jax.ShapeDtypeStruct((n_seq, n_buf, KV_WIDTH), F32),
            jax.ShapeDtypeStruct((n_seq, n_buf, KV_WIDTH), F32),
            jax.ShapeDtypeStruct((n_seq, POOL_BUF, POOL_WIDTH), F32),
        ],
        scratch_shapes=[
            pltpu.VMEM((group, SAMPLE_KEYS, KV_WIDTH), BF16),
            pltpu.VMEM((group, SAMPLE_KEYS, KV_WIDTH), BF16),
            pltpu.VMEM((group, 16 + t_new, POOL_WIDTH), F32),
            pltpu.VMEM((group, t_new, D_MODEL), F32),
        ],
        compiler_params=_params("arbitrary"),
        name="sample_mix",
    )(x, q3, k3, v3, u3, kbuf, vbuf, pbuf, bias, sinkcol, poolw_bd, pscale, wout_b)


def _rms(xf, g):
    ms = jnp.mean(xf * xf, axis=-1, keepdims=True)
    return xf * lax.rsqrt(ms + EPS) * g


def _swiglu(h, wg, wu, wd):
    gate = jnp.dot(h, wg, preferred_element_type=F32)
    up = jnp.dot(h, wu, preferred_element_type=F32)
    act = (gate * jax.nn.sigmoid(gate) * up).astype(BF16)
    return jnp.dot(act, wd, preferred_element_type=F32)


def _ffn_kernel(x_ref, g_ref, wg_ref, wu_ref, wd_ref, o_ref, hb, acc):
    f = pl.program_id(1)

    @pl.when(f == 0)
    def _():
        hb[...] = _rms(x_ref[...], g_ref[...]).astype(BF16)
        acc[...] = jnp.zeros_like(acc)

    acc[...] += _swiglu(hb[...], wg_ref[...], wu_ref[...], wd_ref[...])

    @pl.when(f == pl.num_programs(1) - 1)
    def _():
        o_ref[...] = x_ref[...] + acc[...]


def _ffn(x, g, wg, wu, wd, *, tm, tf):
    n = x.shape[0]
    dff = wg.shape[1]
    row = lambda i, f: (i, 0)
    const = lambda i, f: (0, 0)
    return pl.pallas_call(
        _ffn_kernel,
        grid=(n // tm, dff // tf),
        in_specs=[
            pl.BlockSpec((tm, D_MODEL), row),
            pl.BlockSpec((1, D_MODEL), const),
            pl.BlockSpec((D_MODEL, tf), lambda i, f: (0, f)),
            pl.BlockSpec((D_MODEL, tf), lambda i, f: (0, f)),
            pl.BlockSpec((tf, D_MODEL), lambda i, f: (f, 0)),
        ],
        out_specs=pl.BlockSpec((tm, D_MODEL), row),
        out_shape=jax.ShapeDtypeStruct((n, D_MODEL), F32),
        scratch_shapes=[
            pltpu.VMEM((tm, D_MODEL), BF16),
            pltpu.VMEM((tm, D_MODEL), F32),
        ],
        compiler_params=_params("arbitrary", "arbitrary"),
        name="dense_ffn",
    )(x, g, wg, wu, wd)


MOE_TOK_TILE = 256
MOE_ROW_TILE = 256
FF_CHUNK = 1408


def _route_kernel(x_ref, g_ref, rhi_ref, rlo_ref, rt_ref):
    h = _rms(x_ref[...], g_ref[...])
    hi, lo = _split_bf16(h)
    lg = (jnp.dot(hi, rhi_ref[...], preferred_element_type=F32)
          + jnp.dot(lo, rhi_ref[...], preferred_element_type=F32)
          + jnp.dot(hi, rlo_ref[...], preferred_element_type=F32))
    lane = lax.broadcasted_iota(jnp.int32, lg.shape, 1)
    lg = jnp.where(lane < N_EXPERTS, lg, NEG_INF)
    m1 = jnp.max(lg, axis=-1, keepdims=True)
    i1 = jnp.min(jnp.where(lg == m1, lane, LANES), axis=-1, keepdims=True)
    lg2 = jnp.where(lane == i1, NEG_INF, lg)
    m2 = jnp.max(lg2, axis=-1, keepdims=True)
    i2 = jnp.min(jnp.where(lg2 == m2, lane, LANES), axis=-1, keepdims=True)
    e2 = jnp.exp(m2 - m1)
    den = 1.0 + e2
    rt_ref[...] = jnp.where(lane == 0, i1.astype(F32),
                            jnp.where(lane == 1, i2.astype(F32),
                                      jnp.where(lane == 2, 1.0 / den,
                                                jnp.where(lane == 3, e2 / den, 0.0))))


def _route(x, g, rhi, rlo, *, tm):
    n = x.shape[0]
    row = lambda i: (i, 0)
    const = lambda i: (0, 0)
    return pl.pallas_call(
        _route_kernel,
        grid=(n // tm,),
        in_specs=[
            pl.BlockSpec((tm, D_MODEL), row),
            pl.BlockSpec((1, D_MODEL), const),
            pl.BlockSpec((D_MODEL, LANES), const),
            pl.BlockSpec((D_MODEL, LANES), const),
        ],
        out_specs=pl.BlockSpec((tm, LANES), row),
        out_shape=jax.ShapeDtypeStruct((n, LANES), F32),
        compiler_params=_params("arbitrary"),
        name="moe_route",
    )(x, g, rhi, rlo)


def _row_copies_done(hbm_ref, sem, n_rows):
    pltpu.make_async_copy(hbm_ref.at[pl.ds(0, n_rows), :], hbm_ref.at[pl.ds(0, n_rows), :], sem).wait()


def _dispatch_kernel(pos_ref, x_ref, g_ref, xs_zero_ref, xs_ref, hbuf, sem, *, tm):
    del xs_zero_ref
    i = pl.program_id(0)
    slot = lax.rem(i, 2)

    @pl.when(i >= 2)
    def _():
        _row_copies_done(xs_ref, sem.at[slot], 2 * tm)

    hbuf[slot] = _rms(x_ref[...], g_ref[...])

    def issue_from(s):
        def issue(c, carry):
            t0 = pl.multiple_of(c * 8, 8)
            for j in range(8):
                for k in range(2):
                    pltpu.make_async_copy(hbuf.at[s, pl.ds(t0 + j, 1), :],
                                          xs_ref.at[pl.ds(pos_ref[0, c * 16 + 2 * j + k], 1), :],
                                          sem.at[s]).start()
            return carry
        lax.fori_loop(0, tm // 8, issue, 0)

    for s in range(2):
        pl.when(slot == s)(functools.partial(issue_from, s))

    @pl.when(i == pl.num_programs(0) - 1)
    def _():
        _row_copies_done(xs_ref, sem.at[slot], 2 * tm)

        @pl.when(i >= 1)
        def _():
            _row_copies_done(xs_ref, sem.at[1 - slot], 2 * tm)


def _dispatch(x, g, pos, xs_zero, *, tm):
    n = x.shape[0]
    return pl.pallas_call(
        functools.partial(_dispatch_kernel, tm=tm),
        grid=(n // tm,),
        in_specs=[
            pl.BlockSpec((None, 1, 2 * tm), lambda i: (i, 0, 0), memory_space=pltpu.SMEM),
            pl.BlockSpec((tm, D_MODEL), lambda i: (i, 0)),
            pl.BlockSpec((1, D_MODEL), lambda i: (0, 0)),
            pl.BlockSpec(memory_space=pl.ANY),
        ],
        out_specs=pl.BlockSpec(memory_space=pl.ANY),
        out_shape=jax.ShapeDtypeStruct(xs_zero.shape, F32),
        scratch_shapes=[
            pltpu.VMEM((2, tm, D_MODEL), F32),
            pltpu.SemaphoreType.DMA((2,)),
        ],
        input_output_aliases={3: 0},
        compiler_params=_params("arbitrary"),
        name="moe_dispatch",
    )(pos.reshape(n // tm, 1, 2 * tm), x, g, xs_zero)


def _expert_kernel(te_ref, tv_ref, x_ref, wg_ref, wu_ref, wd_ref, y_ref):
    t = pl.program_id(0)

    @pl.when(tv_ref[t] != 0)
    def _():
        h = x_ref[...].astype(BF16)
        acc = None
        for c0 in range(0, D_FF, FF_CHUNK):
            y = _swiglu(h, wg_ref[:, c0:c0 + FF_CHUNK], wu_ref[:, c0:c0 + FF_CHUNK],
                        wd_ref[c0:c0 + FF_CHUNK, :])
            acc = y if acc is None else acc + y
        y_ref[...] = acc

    @pl.when(tv_ref[t] == 0)
    def _():
        y_ref[...] = jnp.zeros_like(y_ref)


def _experts(xs, tile_expert, tile_valid, wg, wu, wd, *, tr):
    rows = xs.shape[0]
    wmap = lambda t, te, tv: (te[t], 0, 0)
    grid_spec = pltpu.PrefetchScalarGridSpec(
        num_scalar_prefetch=2,
        grid=(rows // tr,),
        in_specs=[
            pl.BlockSpec((tr, D_MODEL), lambda t, te, tv: (t, 0)),
            pl.BlockSpec((None, D_MODEL, D_FF), wmap),
            pl.BlockSpec((None, D_MODEL, D_FF), wmap),
            pl.BlockSpec((None, D_FF, D_MODEL), wmap),
        ],
        out_specs=pl.BlockSpec((tr, D_MODEL), lambda t, te, tv: (t, 0)),
    )
    return pl.pallas_call(
        _expert_kernel,
        grid_spec=grid_spec,
        out_shape=jax.ShapeDtypeStruct((rows, D_MODEL), F32),
        compiler_params=_params("arbitrary"),
        name="moe_experts",
    )(tile_expert, tile_valid, xs, wg, wu, wd)


def _combine_kernel(pos_ref, posn_ref, x_ref, rt_ref, ys_ref, o_ref, ybuf, sem, *, tm):
    i = pl.program_id(0)
    last = pl.num_programs(0) - 1
    slot = lax.rem(i, 2)

    def gather(p_ref, s):
        def issue(c, carry):
            t0 = pl.multiple_of(c * 8, 8)
            for j in range(8):
                for k in range(2):
                    pltpu.make_async_copy(ys_ref.at[pl.ds(p_ref[0, c * 16 + 2 * j + k], 1), :],
                                          ybuf.at[s, k, pl.ds(t0 + j, 1), :],
                                          sem.at[s]).start()
            return carry
        lax.fori_loop(0, tm // 8, issue, 0)

    pl.when(i == 0)(functools.partial(gather, pos_ref, 0))
    for s in range(2):
        pl.when(jnp.logical_and(i < last, slot == s))(functools.partial(gather, posn_ref, 1 - s))

    _row_copies_done(ys_ref, sem.at[slot], 2 * tm)
    rt = rt_ref[...]
    o_ref[...] = x_ref[...] + (rt[:, 2:3] * ybuf[slot, 0] + rt[:, 3:4] * ybuf[slot, 1])


def _combine(x, rt, pos, ys, *, tm):
    n = x.shape[0]
    nt = n // tm
    pos3 = pos.reshape(nt, 1, 2 * tm)
    row = lambda i: (i, 0)
    return pl.pallas_call(
        functools.partial(_combine_kernel, tm=tm),
        grid=(nt,),
        in_specs=[
            pl.BlockSpec((None, 1, 2 * tm), lambda i: (i, 0, 0), memory_space=pltpu.SMEM),
            pl.BlockSpec((None, 1, 2 * tm), lambda i: (jnp.minimum(i + 1, nt - 1), 0, 0),
                         memory_space=pltpu.SMEM),
            pl.BlockSpec((tm, D_MODEL), row),
            pl.BlockSpec((tm, LANES), row),
            pl.BlockSpec(memory_space=pl.ANY),
        ],
        out_specs=pl.BlockSpec((tm, D_MODEL), row),
        out_shape=jax.ShapeDtypeStruct((n, D_MODEL), F32),
        scratch_shapes=[
            pltpu.VMEM((2, 2, tm, D_MODEL), F32),
            pltpu.SemaphoreType.DMA((2,)),
        ],
        compiler_params=_params("arbitrary"),
        name="moe_combine",
    )(pos3, pos3, x, rt, ys)


def _moe(x, g, rhi, rlo, wg, wu, wd):
    n = x.shape[0]
    tm, tr = MOE_TOK_TILE, MOE_ROW_TILE
    rt = _route(x, g, rhi, rlo, tm=tm)
    expert = rt[:, :2].astype(jnp.int32).reshape(2 * n)
    onehot = (expert[:, None] == jnp.arange(N_EXPERTS, dtype=jnp.int32)[None, :]).astype(jnp.int32)
    csum = jnp.cumsum(onehot, axis=0)
    counts = csum[-1]
    group = ((counts + tr - 1) // tr) * tr
    group_end = jnp.cumsum(group)
    pos = jnp.sum(onehot * (csum - 1 + (group_end - group)[None, :]), axis=1).astype(jnp.int32)
    n_rows = 2 * n + N_EXPERTS * tr
    tile_start = jnp.arange(n_rows // tr, dtype=jnp.int32) * tr
    tile_expert = jnp.minimum(jnp.sum(tile_start[:, None] >= group_end[None, :], axis=1),
                              N_EXPERTS - 1).astype(jnp.int32)
    tile_valid = (tile_start < group_end[-1]).astype(jnp.int32)

    xs = _dispatch(x, g, pos, jnp.zeros((n_rows, D_MODEL), F32), tm=tm)
    ys = _experts(xs, tile_expert, tile_valid, wg, wu, wd, tr=tr)
    return _combine(x, rt, pos, ys, tm=tm)


def _head_selectors():
    sel = np.zeros((QK_WIDTH, LANES), np.float32)
    sel[np.arange(QK_WIDTH), np.arange(QK_WIDTH) // HEAD_DIM] = 1.0
    return jnp.asarray(sel, BF16), jnp.asarray(sel.T.copy(), BF16)


def _prompt_bias():
    slopes = _alibi_slopes(N_Q_HEADS)
    qi = np.arange(WINDOW)[:, None]
    kj = np.arange(2 * WINDOW)[None, :]
    dist = (qi + WINDOW - kj).astype(np.float32)
    valid = (dist >= 0) & (dist < WINDOW)
    out = np.empty((N_KV_HEADS, GQA_GROUP * WINDOW, 2 * WINDOW), np.float32)
    for g in range(N_KV_HEADS):
        for r in range(GQA_GROUP):
            b = np.where(valid, -(slopes[GQA_GROUP * g + r] * dist), -np.inf)
            out[g, r * WINDOW:(r + 1) * WINDOW] = b
    return jnp.asarray(out)


def _sample_bias(t_new, n_buf):
    slopes = _alibi_slopes(N_Q_HEADS)
    t = np.arange(t_new)[:, None]
    j = np.arange(SAMPLE_KEYS)[None, :]
    dist = (t + n_buf - j).astype(np.float32)
    valid = (dist >= 0) & (dist < WINDOW) & (j < n_buf + t_new)
    out = np.empty((N_KV_HEADS, SAMPLE_ROWS, SAMPLE_KEYS), np.float32)
    for g in range(N_KV_HEADS):
        for r in range(GQA_GROUP + 1):
            h = GQA_GROUP * g + min(r, GQA_GROUP - 1)
            out[g, r * t_new:(r + 1) * t_new] = np.where(valid, -(slopes[h] * dist), -np.inf)
    return jnp.asarray(out)


def _block_diag_pool(pool_w_l):
    bd = jnp.zeros((POOL_WIDTH, POOL_WIDTH), F32)
    for g in range(len(POOL_WINDOWS)):
        sl = slice(g * POOL_GROUP_DIM, (g + 1) * POOL_GROUP_DIM)
        bd = bd.at[sl, sl].set(pool_w_l[g])
    return bd.astype(BF16)


def kernel(x_prompt, x_sample, state_pool, state_win_k, state_win_v, norm_mix, w_in, q_norm, k_norm,
           attn_sinks, pool_w, pool_scale, w_out, norm_ffn, ffn_w_gate, ffn_w_up, ffn_w_down,
           moe_router, moe_w_gate, moe_w_up, moe_w_down):
    batch, seq, _ = x_prompt.shape
    n_seq, t_new, _ = x_sample.shape
    n_buf = state_win_k.shape[2]
    n_keep = min(WINDOW, seq)
    assert t_new == 8 and n_buf == WINDOW and SAMPLE_ROWS == (GQA_GROUP + 1) * t_new

    sel, selt = _head_selectors()
    bias_p = _prompt_bias()
    bias_s = _sample_bias(t_new, n_buf)

    xp = x_prompt.reshape(batch * seq, D_MODEL)
    xs = x_sample.reshape(n_seq * t_new, D_MODEL)
    pool_p, k_p, v_p, pool_s, k_s, v_s = [], [], [], [], [], []
    for l in range(DEPTH):
        g_mix = norm_mix[l].reshape(1, D_MODEL)
        g_ffn = norm_ffn[l].reshape(1, D_MODEL)
        w_in_b = w_in[l].astype(BF16)
        w_out_b = w_out[l].astype(BF16)
        gain = jnp.concatenate([jnp.tile(q_norm[l], N_Q_HEADS) * (HEAD_DIM ** -0.5),
                                jnp.tile(k_norm[l], N_KV_HEADS)]).reshape(1, QK_WIDTH)
        poolw_bd = _block_diag_pool(pool_w[l])
        pscale = pool_scale[l].reshape(1, POOL_WIDTH)
        sinks = attn_sinks[l].astype(F32)
        sink_rows = jnp.repeat(
            jnp.concatenate([sinks.reshape(N_KV_HEADS, GQA_GROUP), sinks.reshape(N_KV_HEADS, GQA_GROUP)[:, -1:]], axis=1),
            t_new, axis=1)
        sinkcol = jnp.broadcast_to(sink_rows[:, :, None], (N_KV_HEADS, SAMPLE_ROWS, LANES))

        i = l // 2
        if l % 2 == 0:
            wg = ffn_w_gate[i].astype(BF16)
            wu = ffn_w_up[i].astype(BF16)
            wd = ffn_w_down[i].astype(BF16)
            channel_mix = lambda x: _ffn(x, g_ffn, wg, wu, wd, tm=512, tf=1408)
        else:
            wg = moe_w_gate[i].astype(BF16)
            wu = moe_w_up[i].astype(BF16)
            wd = moe_w_down[i].astype(BF16)
            rpad = jnp.pad(moe_router[i], ((0, 0), (0, LANES - N_EXPERTS)))
            rhi, rlo = _split_bf16(rpad)
            channel_mix = lambda x: _moe(x, g_ffn, rhi, rlo, wg, wu, wd)

        u, q, k, v = _project(xp, g_mix, w_in_b, sel, selt, gain, tm=512, q_dtype=BF16)
        xp = _prompt_mix(xp, q, k, v, u, sinks, bias_p, poolw_bd, pscale, w_out_b,
                         batch=batch, seq=seq, tq=512)
        xp = channel_mix(xp)
        pool_p.append(u.reshape(batch, seq, POOL_WIDTH)[:, seq - POOL_BUF:])
        k_p.append(k.reshape(batch, seq, N_KV_HEADS, HEAD_DIM)[:, seq - n_keep:])
        v_p.append(v.reshape(batch, seq, N_KV_HEADS, HEAD_DIM)[:, seq - n_keep:])

        u, q, k, v = _project(xs, g_mix, w_in_b, sel, selt, gain, tm=512, q_dtype=F32)
        xs, ko, vo, po = _sample_mix(
            xs, q, k, v, u,
            state_win_k[l].reshape(n_seq, n_buf, KV_WIDTH),
            state_win_v[l].reshape(n_seq, n_buf, KV_WIDTH),
            state_pool[l], bias_s, sinkcol, poolw_bd, pscale, w_out_b,
            n_seq=n_seq, t_new=t_new, group=16)
        xs = channel_mix(xs)
        pool_s.append(po)
        k_s.append(ko.reshape(n_seq, n_buf, N_KV_HEADS, HEAD_DIM))
        v_s.append(vo.reshape(n_seq, n_buf, N_KV_HEADS, HEAD_DIM))

    return (xp.reshape(batch, seq, D_MODEL), xs.reshape(n_seq, t_new, D_MODEL),
            jnp.stack(pool_p), jnp.stack(k_p), jnp.stack(v_p),
            jnp.stack(pool_s), jnp.stack(k_s), jnp.stack(v_s))
```

```python
import functools
import math

import numpy as np
import jax
import jax.numpy as jnp
from jax import lax
from jax.experimental import pallas as pl
from jax.experimental.pallas import tpu as pltpu

F32 = jnp.float32
BF16 = jnp.bfloat16

D_MODEL = 1024
DEPTH = 2
POOL_WIDTH = 256
POOL_GROUP_DIM = 64
POOL_WINDOWS = (2, 4, 8, 16)
POOL_BUF = 15
HEAD_DIM = 64
N_Q_HEADS = 12
N_KV_HEADS = 4
GQA_GROUP = 3
ATTN_WIDTH = 768
KV_WIDTH = 256
QK_WIDTH = ATTN_WIDTH + KV_WIDTH
IN_WIDTH = 1536
WINDOW = 128
D_FF = 2816
N_EXPERTS = 8
EPS = 1e-6
NEG_INF = float("-inf")

LANES = 128
VMEM_LIMIT = 56 * 1024 * 1024


def _alibi_slopes(n):
    def pow2_slopes(m):
        start = 2.0 ** (-8.0 / m)
        return [start ** (i + 1) for i in range(m)]
    if float(math.log2(n)).is_integer():
        s = pow2_slopes(n)
    else:
        c = 2 ** int(math.floor(math.log2(n)))
        s = pow2_slopes(c) + pow2_slopes(2 * c)[0::2][: n - c]
    return np.array(s, dtype=np.float32)


def _split_bf16(x):
    hi = x.astype(BF16)
    lo = (x - hi.astype(F32)).astype(BF16)
    return hi, lo


def _params(*sem):
    return pltpu.CompilerParams(dimension_semantics=sem, vmem_limit_bytes=VMEM_LIMIT)


def _proj_kernel(x_ref, g_ref, w_ref, sel_ref, selt_ref, gain_ref,
                 u_ref, q_ref, k_ref, v_ref):
    xf = x_ref[...]
    ms = jnp.mean(xf * xf, axis=-1, keepdims=True)
    h = (xf * lax.rsqrt(ms + EPS) * g_ref[...]).astype(BF16)
    proj = jnp.dot(h, w_ref[...], preferred_element_type=F32)
    u_ref[...] = proj[:, :POOL_WIDTH]
    v_ref[...] = proj[:, POOL_WIDTH + QK_WIDTH:]
    qk = proj[:, POOL_WIDTH:POOL_WIDTH + QK_WIDTH]
    hi, lo = _split_bf16(qk * qk)
    ss = (jnp.dot(hi, sel_ref[...], preferred_element_type=F32)
          + jnp.dot(lo, sel_ref[...], preferred_element_type=F32))
    scale = lax.rsqrt(ss * (1.0 / HEAD_DIM) + EPS)
    shi, slo = _split_bf16(scale)
    full = (jnp.dot(shi, selt_ref[...], preferred_element_type=F32)
            + jnp.dot(slo, selt_ref[...], preferred_element_type=F32))
    qkn = qk * full * gain_ref[...]
    q_ref[...] = qkn[:, :ATTN_WIDTH].astype(q_ref.dtype)
    k_ref[...] = qkn[:, ATTN_WIDTH:]


def _project(x, g, w_in_b, sel, selt, gain, *, tm, q_dtype):
    n = x.shape[0]
    const = lambda i: (0, 0)
    row = lambda i: (i, 0)
    return pl.pallas_call(
        _proj_kernel,
        grid=(n // tm,),
        in_specs=[
            pl.BlockSpec((tm, D_MODEL), row),
            pl.BlockSpec((1, D_MODEL), const),
            pl.BlockSpec((D_MODEL, IN_WIDTH), const),
            pl.BlockSpec((QK_WIDTH, LANES), const),
            pl.BlockSpec((LANES, QK_WIDTH), const),
            pl.BlockSpec((1, QK_WIDTH), const),
        ],
        out_specs=[
            pl.BlockSpec((tm, POOL_WIDTH), row),
            pl.BlockSpec((tm, ATTN_WIDTH), row),
            pl.BlockSpec((tm, KV_WIDTH), row),
            pl.BlockSpec((tm, KV_WIDTH), row),
        ],
        out_shape=[
            jax.ShapeDtypeStruct((n, POOL_WIDTH), F32),
            jax.ShapeDtypeStruct((n, ATTN_WIDTH), q_dtype),
            jax.ShapeDtypeStruct((n, KV_WIDTH), F32),
            jax.ShapeDtypeStruct((n, KV_WIDTH), F32),
        ],
        compiler_params=_params("arbitrary"),
        name="norm_proj",
    )(x, g, w_in_b, sel, selt, gain)


def _pool_window_sums(load_shifted, lane_axis=1):
    x0 = load_shifted(0)
    acc = x0 + load_shifted(1)
    s2 = acc
    for d in range(2, 4):
        acc = acc + load_shifted(d)
    s4 = acc
    for d in range(4, 8):
        acc = acc + load_shifted(d)
    s8 = acc
    for d in range(8, 16):
        acc = acc + load_shifted(d)
    s16 = acc
    lane = lax.broadcasted_iota(jnp.int32, x0.shape, lane_axis)
    win = jnp.where(lane < 64, s2, jnp.where(lane < 128, s4, jnp.where(lane < 192, s8, s16)))
    return x0, win, lane


def _pool_window_len(lane):
    return jnp.where(lane < 64, 2, jnp.where(lane < 128, 4, jnp.where(lane < 192, 8, 16)))


def _prompt_mix_kernel(sink_ref, x_ref, q_ref, kc_ref, kp_ref, vc_ref, vp_ref, uc_ref, up_ref,
                       bias_ref, poolw_ref, pscale_ref, wout_ref,
                       o_ref, kk, vv, uu, mix, *, tq):
    i = pl.program_id(1)
    first = i == 0
    blk = WINDOW
    kk[0:blk] = kp_ref[...].astype(BF16)
    kk[blk:] = kc_ref[...].astype(BF16)
    vv[0:blk] = vp_ref[...].astype(BF16)
    vv[blk:] = vc_ref[...].astype(BF16)
    uu[0:16] = jnp.where(first, 0.0, up_ref[...])
    uu[16:] = uc_ref[...]

    col = lax.broadcasted_iota(jnp.int32, (1, 2 * blk), 1)
    no_prev = jnp.where(jnp.logical_and(first, col < blk), NEG_INF, 0.0)

    for j in range(tq // blk):
        r0 = j * blk
        x0, win, lane = _pool_window_sums(lambda d: uu[16 + r0 - d:16 + r0 - d + blk, :])
        pos = i * tq + r0 + lax.broadcasted_iota(jnp.int32, x0.shape, 0)
        cnt = jnp.minimum(_pool_window_len(lane), pos + 1).astype(F32)
        pooled = (win / cnt - x0).astype(BF16)
        pm = jnp.dot(pooled, poolw_ref[...], preferred_element_type=F32) * pscale_ref[...]
        mix[r0:r0 + blk, 0:POOL_WIDTH] = pm.astype(BF16)
        upper = lax.broadcasted_iota(jnp.int32, (2 * blk, LANES), 1) >= HEAD_DIM
        upper_q = lax.broadcasted_iota(jnp.int32, (blk, LANES), 1) >= HEAD_DIM
        res_even = None
        for g in range(N_KV_HEADS):
            c = (g // 2) * LANES
            kcol = kk[r0:r0 + 2 * blk, c:c + LANES]
            vcol = vv[r0:r0 + 2 * blk, c:c + LANES]
            kswap = jnp.concatenate([kcol[:, HEAD_DIM:], kcol[:, :HEAD_DIM]], axis=1)
            vswap = jnp.concatenate([vcol[:, HEAD_DIM:], vcol[:, :HEAD_DIM]], axis=1)
            k_half, v_half = [], []
            for par in range(2):
                in_half = upper if par == 1 else jnp.logical_not(upper)
                ksrc, vsrc = (kcol, vcol) if par == g % 2 else (kswap, vswap)
                k_half.append(jnp.where(in_half, ksrc, jnp.zeros_like(ksrc)))
                v_half.append(jnp.where(in_half, vsrc, jnp.ones_like(vsrc)))
            for r in range(GQA_GROUP):
                h = GQA_GROUP * g + r
                par = h % 2
                qc = (h // 2) * LANES
                s = lax.dot_general(q_ref[r0:r0 + blk, qc:qc + LANES], k_half[par],
                                    (((1,), (1,)), ((), ())), preferred_element_type=F32)
                s = s + bias_ref[g, r * blk:(r + 1) * blk, :]
                if j == 0:
                    s = s + no_prev
                sink = sink_ref[h]
                m = jnp.maximum(jnp.max(s, axis=-1, keepdims=True), sink)
                p = jnp.exp(s - m).astype(BF16)
                oa = jnp.dot(p, v_half[par], preferred_element_type=F32)
                sum_lane = HEAD_DIM if par == 0 else 0
                res = oa / (oa[:, sum_lane:sum_lane + 1] + jnp.exp(sink - m))
                if par == 0:
                    res_even = res
                else:
                    c0 = POOL_WIDTH + qc
                    mix[r0:r0 + blk, c0:c0 + LANES] = jnp.where(upper_q, res, res_even).astype(BF16)
        o_ref[r0:r0 + blk, :] = x_ref[r0:r0 + blk, :] + jnp.dot(
            mix[r0:r0 + blk, :], wout_ref[...], preferred_element_type=F32)


def _prompt_mix(x, q, k, v, u, sinks, bias, poolw_bd, pscale, wout_b, *, batch, seq, tq):
    nb = seq // tq
    x3 = x.reshape(batch, seq, D_MODEL)
    q3 = q.reshape(batch, seq, ATTN_WIDTH)
    k3 = k.reshape(batch, seq, KV_WIDTH)
    v3 = v.reshape(batch, seq, KV_WIDTH)
    u3 = u.reshape(batch, seq, POOL_WIDTH)
    cur = lambda b, i: (b, i, 0)
    prev_kv = lambda b, i: (b, jnp.maximum(i * (tq // WINDOW) - 1, 0), 0)
    prev_u = lambda b, i: (b, jnp.maximum(i * (tq // 16) - 1, 0), 0)
    const2 = lambda b, i: (0, 0)
    const3 = lambda b, i: (0, 0, 0)
    out = pl.pallas_call(
        functools.partial(_prompt_mix_kernel, tq=tq),
        grid=(batch, nb),
        in_specs=[
            pl.BlockSpec(memory_space=pltpu.SMEM),
            pl.BlockSpec((None, tq, D_MODEL), cur),
            pl.BlockSpec((None, tq, ATTN_WIDTH), cur),
            pl.BlockSpec((None, tq, KV_WIDTH), cur),
            pl.BlockSpec((None, WINDOW, KV_WIDTH), prev_kv),
            pl.BlockSpec((None, tq, KV_WIDTH), cur),
            pl.BlockSpec((None, WINDOW, KV_WIDTH), prev_kv),
            pl.BlockSpec((None, tq, POOL_WIDTH), cur),
            pl.BlockSpec((None, 16, POOL_WIDTH), prev_u),
            pl.BlockSpec((N_KV_HEADS, GQA_GROUP * WINDOW, 2 * WINDOW), const3),
            pl.BlockSpec((POOL_WIDTH, POOL_WIDTH), const2),
            pl.BlockSpec((1, POOL_WIDTH), const2),
            pl.BlockSpec((D_MODEL, D_MODEL), const2),
        ],
        out_specs=pl.BlockSpec((None, tq, D_MODEL), cur),
        out_shape=jax.ShapeDtypeStruct((batch, seq, D_MODEL), F32),
        scratch_shapes=[
            pltpu.VMEM((tq + WINDOW, KV_WIDTH), BF16),
            pltpu.VMEM((tq + WINDOW, KV_WIDTH), BF16),
            pltpu.VMEM((tq + 16, POOL_WIDTH), F32),
            pltpu.VMEM((tq, D_MODEL), BF16),
        ],
        compiler_params=_params("arbitrary", "arbitrary"),
        name="prompt_mix",
    )(sinks, x3, q3, k3, k3, v3, v3, u3, u3, bias, poolw_bd, pscale, wout_b)
    return out.reshape(batch * seq, D_MODEL)


SAMPLE_ROWS = 32
SAMPLE_KEYS = 256


def _sample_mix_kernel(x_ref, q_ref, kn_ref, vn_ref, un_ref, kb_ref, vb_ref, pb_ref,
                       bias_ref, sinkcol_ref, poolw_ref, pscale_ref, wout_ref,
                       o_ref, ko_ref, vo_ref, po_ref,
                       kx, vx, px, mix, *, group, t_new, n_buf):
    rows = group * t_new
    ko_ref[:, 0:n_buf - t_new, :] = kb_ref[:, t_new:n_buf, :]
    ko_ref[:, n_buf - t_new:n_buf, :] = kn_ref[...]
    vo_ref[:, 0:n_buf - t_new, :] = vb_ref[:, t_new:n_buf, :]
    vo_ref[:, n_buf - t_new:n_buf, :] = vn_ref[...]
    pad8 = jnp.zeros((group, 16 - t_new, KV_WIDTH), F32)
    tail = jnp.zeros((group, SAMPLE_KEYS - n_buf - 16, KV_WIDTH), BF16)
    kx[:, 0:n_buf, :] = kb_ref[...].astype(BF16)
    kx[:, n_buf:n_buf + 16, :] = jnp.concatenate([kn_ref[...], pad8], axis=1).astype(BF16)
    kx[:, n_buf + 16:, :] = tail
    vx[:, 0:n_buf, :] = vb_ref[...].astype(BF16)
    vx[:, n_buf:n_buf + 16, :] = jnp.concatenate([vn_ref[...], pad8], axis=1).astype(BF16)
    vx[:, n_buf + 16:, :] = tail

    px[:, 1:16, :] = pb_ref[...]
    px[:, 16:16 + t_new, :] = un_ref[...]
    po_ref[...] = px[:, 16 + t_new - POOL_BUF:16 + t_new, :]
    x0, win, lane = _pool_window_sums(lambda d: px[:, 16 - d:16 - d + t_new, :], lane_axis=2)
    cnt = _pool_window_len(lane).astype(F32)
    pooled = (win / cnt - x0).reshape(rows, POOL_WIDTH).astype(BF16)
    pm = jnp.dot(pooled, poolw_ref[...], preferred_element_type=F32) * pscale_ref[...]
    mix[:, :, 0:POOL_WIDTH] = pm.reshape(group, t_new, POOL_WIDTH)

    q3 = q_ref[...]
    for g in range(N_KV_HEADS):
        heads = [q3[:, :, HEAD_DIM * (GQA_GROUP * g + r):HEAD_DIM * (GQA_GROUP * g + r + 1)]
                 for r in range(GQA_GROUP)]
        qs = jnp.concatenate(heads + [heads[-1]], axis=1).astype(BF16)
        kh = kx[:, :, HEAD_DIM * g:HEAD_DIM * (g + 1)]
        vh = vx[:, :, HEAD_DIM * g:HEAD_DIM * (g + 1)]
        s = jnp.einsum("gqd,gkd->gqk", qs, kh, preferred_element_type=F32) + bias_ref[g][None]
        sink = sinkcol_ref[g][:, 0:1][None]
        m = jnp.maximum(jnp.max(s, axis=-1, keepdims=True), sink)
        p = jnp.exp(s - m)
        den = jnp.sum(p, axis=-1, keepdims=True) + jnp.exp(sink - m)
        o = jnp.einsum("gqk,gkd->gqd", p.astype(BF16), vh, preferred_element_type=F32) / den
        for r in range(GQA_GROUP):
            c0 = POOL_WIDTH + HEAD_DIM * (GQA_GROUP * g + r)
            mix[:, :, c0:c0 + HEAD_DIM] = o[:, r * t_new:(r + 1) * t_new, :]

    mixb = mix[...].reshape(rows, D_MODEL).astype(BF16)
    o_ref[...] = x_ref[...] + jnp.dot(mixb, wout_ref[...], preferred_element_type=F32)


def _sample_mix(x, q, k, v, u, kbuf, vbuf, pbuf, bias, sinkcol, poolw_bd, pscale, wout_b,
                *, n_seq, t_new, group):
    n_buf = kbuf.shape[1]
    q3 = q.reshape(n_seq, t_new, ATTN_WIDTH)
    k3 = k.reshape(n_seq, t_new, KV_WIDTH)
    v3 = v.reshape(n_seq, t_new, KV_WIDTH)
    u3 = u.reshape(n_seq, t_new, POOL_WIDTH)
    rows = group * t_new
    seq3 = lambda i: (i, 0, 0)
    row2 = lambda i: (i, 0)
    const2 = lambda i: (0, 0)
    const3 = lambda i: (0, 0, 0)
    return pl.pallas_call(
        functools.partial(_sample_mix_kernel, group=group, t_new=t_new, n_buf=n_buf),
        grid=(n_seq // group,),
        in_specs=[
            pl.BlockSpec((rows, D_MODEL), row2),
            pl.BlockSpec((group, t_new, ATTN_WIDTH), seq3),
            pl.BlockSpec((group, t_new, KV_WIDTH), seq3),
            pl.BlockSpec((group, t_new, KV_WIDTH), seq3),
            pl.BlockSpec((group, t_new, POOL_WIDTH), seq3),
            pl.BlockSpec((group, n_buf, KV_WIDTH), seq3),
            pl.BlockSpec((group, n_buf, KV_WIDTH), seq3),
            pl.BlockSpec((group, POOL_BUF, POOL_WIDTH), seq3),
            pl.BlockSpec((N_KV_HEADS, SAMPLE_ROWS, SAMPLE_KEYS), const3),
            pl.BlockSpec((N_KV_HEADS, SAMPLE_ROWS, LANES), const3),
            pl.BlockSpec((POOL_WIDTH, POOL_WIDTH), const2),
            pl.BlockSpec((1, POOL_WIDTH), const2),
            pl.BlockSpec((D_MODEL, D_MODEL), const2),
        ],
        out_specs=[
            pl.BlockSpec((rows, D_MODEL), row2),
            pl.BlockSpec((group, n_buf, KV_WIDTH), seq3),
            pl.BlockSpec((group, n_buf, KV_WIDTH), seq3),
            pl.BlockSpec((group, POOL_BUF, POOL_WIDTH), seq3),
        ],
        out_shape=[
            jax.ShapeDtypeStruct((n_seq * t_new, D_MODEL), F32),
            jax.ShapeDtypeStruct((n_seq, n_buf, KV_WIDTH), F32),
            jax.ShapeDtypeStruct((n_seq, n_buf, KV_WIDTH), F32),
            jax.ShapeDtypeStruct((n_seq, POOL_BUF, POOL_WIDTH), F32),
        ],
        scratch_shapes=[
            pltpu.VMEM((group, SAMPLE_KEYS, KV_WIDTH), BF16),
            pltpu.VMEM((group, SAMPLE_KEYS, KV_WIDTH), BF16),
            pltpu.VMEM((group, 16 + t_new, POOL_WIDTH), F32),
            pltpu.VMEM((group, t_new, D_MODEL), F32),
        ],
        compiler_params=_params("arbitrary"),
        name="sample_mix",
    )(x, q3, k3, v3, u3, kbuf, vbuf, pbuf, bias, sinkcol, poolw_bd, pscale, wout_b)


def _rms(xf, g):
    ms = jnp.mean(xf * xf, axis=-1, keepdims=True)
    return xf * lax.rsqrt(ms + EPS) * g


def _swiglu(h, wg, wu, wd):
    gate = jnp.dot(h, wg, preferred_element_type=F32)
    up = jnp.dot(h, wu, preferred_element_type=F32)
    act = (gate * jax.nn.sigmoid(gate) * up).astype(BF16)
    return jnp.dot(act, wd, preferred_element_type=F32)


def _ffn_kernel(x_ref, g_ref, wg_ref, wu_ref, wd_ref, o_ref, hb, acc):
    f = pl.program_id(1)

    @pl.when(f == 0)
    def _():
        hb[...] = _rms(x_ref[...], g_ref[...]).astype(BF16)
        acc[...] = jnp.zeros_like(acc)

    acc[...] += _swiglu(hb[...], wg_ref[...], wu_ref[...], wd_ref[...])

    @pl.when(f == pl.num_programs(1) - 1)
    def _():
        o_ref[...] = x_ref[...] + acc[...]


def _ffn(x, g, wg, wu, wd, *, tm, tf):
    n = x.shape[0]
    dff = wg.shape[1]
    row = lambda i, f: (i, 0)
    const = lambda i, f: (0, 0)
    return pl.pallas_call(
        _ffn_kernel,
        grid=(n // tm, dff // tf),
        in_specs=[
            pl.BlockSpec((tm, D_MODEL), row),
            pl.BlockSpec((1, D_MODEL), const),
            pl.BlockSpec((D_MODEL, tf), lambda i, f: (0, f)),
            pl.BlockSpec((D_MODEL, tf), lambda i, f: (0, f)),
            pl.BlockSpec((tf, D_MODEL), lambda i, f: (f, 0)),
        ],
        out_specs=pl.BlockSpec((tm, D_MODEL), row),
        out_shape=jax.ShapeDtypeStruct((n, D_MODEL), F32),
        scratch_shapes=[
            pltpu.VMEM((tm, D_MODEL), BF16),
            pltpu.VMEM((tm, D_MODEL), F32),
        ],
        compiler_params=_params("arbitrary", "arbitrary"),
        name="dense_ffn",
    )(x, g, wg, wu, wd)


MOE_TOK_TILE = 256
MOE_ROW_TILE = 256
FF_CHUNK = 1408


def _route_kernel(x_ref, g_ref, rhi_ref, rlo_ref, rt_ref):
    h = _rms(x_ref[...], g_ref[...])
    hi, lo = _split_bf16(h)
    lg = (jnp.dot(hi, rhi_ref[...], preferred_element_type=F32)
          + jnp.dot(lo, rhi_ref[...], preferred_element_type=F32)
          + jnp.dot(hi, rlo_ref[...], preferred_element_type=F32))
    lane = lax.broadcasted_iota(jnp.int32, lg.shape, 1)
    lg = jnp.where(lane < N_EXPERTS, lg, NEG_INF)
    m1 = jnp.max(lg, axis=-1, keepdims=True)
    i1 = jnp.min(jnp.where(lg == m1, lane, LANES), axis=-1, keepdims=True)
    lg2 = jnp.where(lane == i1, NEG_INF, lg)
    m2 = jnp.max(lg2, axis=-1, keepdims=True)
    i2 = jnp.min(jnp.where(lg2 == m2, lane, LANES), axis=-1, keepdims=True)
    e2 = jnp.exp(m2 - m1)
    den = 1.0 + e2
    rt_ref[...] = jnp.where(lane == 0, i1.astype(F32),
                            jnp.where(lane == 1, i2.astype(F32),
                                      jnp.where(lane == 2, 1.0 / den,
                                                jnp.where(lane == 3, e2 / den, 0.0))))


def _route(x, g, rhi, rlo, *, tm):
    n = x.shape[0]
    row = lambda i: (i, 0)
    const = lambda i: (0, 0)
    return pl.pallas_call(
        _route_kernel,
        grid=(n // tm,),
        in_specs=[
            pl.BlockSpec((tm, D_MODEL), row),
            pl.BlockSpec((1, D_MODEL), const),
            pl.BlockSpec((D_MODEL, LANES), const),
            pl.BlockSpec((D_MODEL, LANES), const),
        ],
        out_specs=pl.BlockSpec((tm, LANES), row),
        out_shape=jax.ShapeDtypeStruct((n, LANES), F32),
        compiler_params=_params("arbitrary"),
        name="moe_route",
    )(x, g, rhi, rlo)


def _row_copies_done(hbm_ref, sem, n_rows):
    pltpu.make_async_copy(hbm_ref.at[pl.ds(0, n_rows), :], hbm_ref.at[pl.ds(0, n_rows), :], sem).wait()


def _dispatch_kernel(pos_ref, x_ref, g_ref, xs_zero_ref, xs_ref, hbuf, sem, *, tm):
    del xs_zero_ref
    i = pl.program_id(0)
    slot = lax.rem(i, 2)

    @pl.when(i >= 2)
    def _():
        _row_copies_done(xs_ref, sem.at[slot], 2 * tm)

    hbuf[slot] = _rms(x_ref[...], g_ref[...])

    def issue_from(s):
        def issue(c, carry):
            t0 = pl.multiple_of(c * 8, 8)
            for j in range(8):
                for k in range(2):
                    pltpu.make_async_copy(hbuf.at[s, pl.ds(t0 + j, 1), :],
                                          xs_ref.at[pl.ds(pos_ref[0, c * 16 + 2 * j + k], 1), :],
                                          sem.at[s]).start()
            return carry
        lax.fori_loop(0, tm // 8, issue, 0)

    for s in range(2):
        pl.when(slot == s)(functools.partial(issue_from, s))

    @pl.when(i == pl.num_programs(0) - 1)
    def _():
        _row_copies_done(xs_ref, sem.at[slot], 2 * tm)

        @pl.when(i >= 1)
        def _():
            _row_copies_done(xs_ref, sem.at[1 - slot], 2 * tm)


def _dispatch(x, g, pos, xs_zero, *, tm):
    n = x.shape[0]
    return pl.pallas_call(
        functools.partial(_dispatch_kernel, tm=tm),
        grid=(n // tm,),
        in_specs=[
            pl.BlockSpec((None, 1, 2 * tm), lambda i: (i, 0, 0), memory_space=pltpu.SMEM),
            pl.BlockSpec((tm, D_MODEL), lambda i: (i, 0)),
            pl.BlockSpec((1, D_MODEL), lambda i: (0, 0)),
            pl.BlockSpec(memory_space=pl.ANY),
        ],
        out_specs=pl.BlockSpec(memory_space=pl.ANY),
        out_shape=jax.ShapeDtypeStruct(xs_zero.shape, F32),
        scratch_shapes=[
            pltpu.VMEM((2, tm, D_MODEL), F32),
            pltpu.SemaphoreType.DMA((2,)),
        ],
        input_output_aliases={3: 0},
        compiler_params=_params("arbitrary"),
        name="moe_dispatch",
    )(pos.reshape(n // tm, 1, 2 * tm), x, g, xs_zero)


def _expert_kernel(te_ref, tv_ref, x_ref, wg_ref, wu_ref, wd_ref, y_ref):
    t = pl.program_id(0)

    @pl.when(tv_ref[t] != 0)
    def _():
        h = x_ref[...].astype(BF16)
        acc = None
        for c0 in range(0, D_FF, FF_CHUNK):
            y = _swiglu(h, wg_ref[:, c0:c0 + FF_CHUNK], wu_ref[:, c0:c0 + FF_CHUNK],
                        wd_ref[c0:c0 + FF_CHUNK, :])
            acc = y if acc is None else acc + y
        y_ref[...] = acc

    @pl.when(tv_ref[t] == 0)
    def _():
        y_ref[...] = jnp.zeros_like(y_ref)


def _experts(xs, tile_expert, tile_valid, wg, wu, wd, *, tr):
    rows = xs.shape[0]
    wmap = lambda t, te, tv: (te[t], 0, 0)
    grid_spec = pltpu.PrefetchScalarGridSpec(
        num_scalar_prefetch=2,
        grid=(rows // tr,),
        in_specs=[
            pl.BlockSpec((tr, D_MODEL), lambda t, te, tv: (t, 0)),
            pl.BlockSpec((None, D_MODEL, D_FF), wmap),
            pl.BlockSpec((None, D_MODEL, D_FF), wmap),
            pl.BlockSpec((None, D_FF, D_MODEL), wmap),
        ],
        out_specs=pl.BlockSpec((tr, D_MODEL), lambda t, te, tv: (t, 0)),
    )
    return pl.pallas_call(
        _expert_kernel,
        grid_spec=grid_spec,
        out_shape=jax.ShapeDtypeStruct((rows, D_MODEL), F32),
        compiler_params=_params("arbitrary"),
        name="moe_experts",
    )(tile_expert, tile_valid, xs, wg, wu, wd)


def _combine_kernel(pos_ref, posn_ref, x_ref, rt_ref, ys_ref, o_ref, ybuf, sem, *, tm):
    i = pl.program_id(0)
    last = pl.num_programs(0) - 1
    slot = lax.rem(i, 2)

    def gather(p_ref, s):
        def issue(c, carry):
            t0 = pl.multiple_of(c * 8, 8)
            for j in range(8):
                for k in range(2):
                    pltpu.make_async_copy(ys_ref.at[pl.ds(p_ref[0, c * 16 + 2 * j + k], 1), :],
                                          ybuf.at[s, k, pl.ds(t0 + j, 1), :],
                                          sem.at[s]).start()
            return carry
        lax.fori_loop(0, tm // 8, issue, 0)

    pl.when(i == 0)(functools.partial(gather, pos_ref, 0))
    for s in range(2):
        pl.when(jnp.logical_and(i < last, slot == s))(functools.partial(gather, posn_ref, 1 - s))

    _row_copies_done(ys_ref, sem.at[slot], 2 * tm)
    rt = rt_ref[...]
    o_ref[...] = x_ref[...] + (rt[:, 2:3] * ybuf[slot, 0] + rt[:, 3:4] * ybuf[slot, 1])


def _combine(x, rt, pos, ys, *, tm):
    n = x.shape[0]
    nt = n // tm
    pos3 = pos.reshape(nt, 1, 2 * tm)
    row = lambda i: (i, 0)
    return pl.pallas_call(
        functools.partial(_combine_kernel, tm=tm),
        grid=(nt,),
        in_specs=[
            pl.BlockSpec((None, 1, 2 * tm), lambda i: (i, 0, 0), memory_space=pltpu.SMEM),
            pl.BlockSpec((None, 1, 2 * tm), lambda i: (jnp.minimum(i + 1, nt - 1), 0, 0),
                         memory_space=pltpu.SMEM),
            pl.BlockSpec((tm, D_MODEL), row),
            pl.BlockSpec((tm, LANES), row),
            pl.BlockSpec(memory_space=pl.ANY),
        ],
        out_specs=pl.BlockSpec((tm, D_MODEL), row),
        out_shape=jax.ShapeDtypeStruct((n, D_MODEL), F32),
        scratch_shapes=[
            pltpu.VMEM((2, 2, tm, D_MODEL), F32),
            pltpu.SemaphoreType.DMA((2,)),
        ],
        compiler_params=_params("arbitrary"),
        name="moe_combine",
    )(pos3, pos3, x, rt, ys)


def _moe(x, g, rhi, rlo, wg, wu, wd):
    n = x.shape[0]
    tm, tr = MOE_TOK_TILE, MOE_ROW_TILE
    rt = _route(x, g, rhi, rlo, tm=tm)
    expert = rt[:, :2].astype(jnp.int32).reshape(2 * n)
    onehot = (expert[:, None] == jnp.arange(N_EXPERTS, dtype=jnp.int32)[None, :]).astype(jnp.int32)
    csum = jnp.cumsum(onehot, axis=0)
    counts = csum[-1]
    group = ((counts + tr - 1) // tr) * tr
    group_end = jnp.cumsum(group)
    pos = jnp.sum(onehot * (csum - 1 + (group_end - group)[None, :]), axis=1).astype(jnp.int32)
    n_rows = 2 * n + N_EXPERTS * tr
    tile_start = jnp.arange(n_rows // tr, dtype=jnp.int32) * tr
    tile_expert = jnp.minimum(jnp.sum(tile_start[:, None] >= group_end[None, :], axis=1),
                              N_EXPERTS - 1).astype(jnp.int32)
    tile_valid = (tile_start < group_end[-1]).astype(jnp.int32)

    xs = _dispatch(x, g, pos, jnp.zeros((n_rows, D_MODEL), F32), tm=tm)
    ys = _experts(xs, tile_expert, tile_valid, wg, wu, wd, tr=tr)
    return _combine(x, rt, pos, ys, tm=tm)


def _head_selectors():
    sel = np.zeros((QK_WIDTH, LANES), np.float32)
    sel[np.arange(QK_WIDTH), np.arange(QK_WIDTH) // HEAD_DIM] = 1.0
    return jnp.asarray(sel, BF16), jnp.asarray(sel.T.copy(), BF16)


def _prompt_bias():
    slopes = _alibi_slopes(N_Q_HEADS)
    qi = np.arange(WINDOW)[:, None]
    kj = np.arange(2 * WINDOW)[None, :]
    dist = (qi + WINDOW - kj).astype(np.float32)
    valid = (dist >= 0) & (dist < WINDOW)
    out = np.empty((N_KV_HEADS, GQA_GROUP * WINDOW, 2 * WINDOW), np.float32)
    for g in range(N_KV_HEADS):
        for r in range(GQA_GROUP):
            b = np.where(valid, -(slopes[GQA_GROUP * g + r] * dist), -np.inf)
            out[g, r * WINDOW:(r + 1) * WINDOW] = b
    return jnp.asarray(out)


def _sample_bias(t_new, n_buf):
    slopes = _alibi_slopes(N_Q_HEADS)
    t = np.arange(t_new)[:, None]
    j = np.arange(SAMPLE_KEYS)[None, :]
    dist = (t + n_buf - j).astype(np.float32)
    valid = (dist >= 0) & (dist < WINDOW) & (j < n_buf + t_new)
    out = np.empty((N_KV_HEADS, SAMPLE_ROWS, SAMPLE_KEYS), np.float32)
    for g in range(N_KV_HEADS):
        for r in range(GQA_GROUP + 1):
            h = GQA_GROUP * g + min(r, GQA_GROUP - 1)
            out[g, r * t_new:(r + 1) * t_new] = np.where(valid, -(slopes[h] * dist), -np.inf)
    return jnp.asarray(out)


def _block_diag_pool(pool_w_l):
    bd = jnp.zeros((POOL_WIDTH, POOL_WIDTH), F32)
    for g in range(len(POOL_WINDOWS)):
        sl = slice(g * POOL_GROUP_DIM, (g + 1) * POOL_GROUP_DIM)
        bd = bd.at[sl, sl].set(pool_w_l[g])
    return bd.astype(BF16)


def kernel(x_prompt, x_sample, state_pool, state_win_k, state_win_v, norm_mix, w_in, q_norm, k_norm,
           attn_sinks, pool_w, pool_scale, w_out, norm_ffn, ffn_w_gate, ffn_w_up, ffn_w_down,
           moe_router, moe_w_gate, moe_w_up, moe_w_down):
    batch, seq, _ = x_prompt.shape
    n_seq, t_new, _ = x_sample.shape
    n_buf = state_win_k.shape[2]
    n_keep = min(WINDOW, seq)
    assert t_new == 8 and n_buf == WINDOW and SAMPLE_ROWS == (GQA_GROUP + 1) * t_new

    sel, selt = _head_selectors()
    bias_p = _prompt_bias()
    bias_s = _sample_bias(t_new, n_buf)

    xp = x_prompt.reshape(batch * seq, D_MODEL)
    xs = x_sample.reshape(n_seq * t_new, D_MODEL)
    pool_p, k_p, v_p, pool_s, k_s, v_s = [], [], [], [], [], []
    for l in range(DEPTH):
        g_mix = norm_mix[l].reshape(1, D_MODEL)
        g_ffn = norm_ffn[l].reshape(1, D_MODEL)
        w_in_b = w_in[l].astype(BF16)
        w_out_b = w_out[l].astype(BF16)
        gain = jnp.concatenate([jnp.tile(q_norm[l], N_Q_HEADS) * (HEAD_DIM ** -0.5),
                                jnp.tile(k_norm[l], N_KV_HEADS)]).reshape(1, QK_WIDTH)
        poolw_bd = _block_diag_pool(pool_w[l])
        pscale = pool_scale[l].reshape(1, POOL_WIDTH)
        sinks = attn_sinks[l].astype(F32)
        sink_rows = jnp.repeat(
            jnp.concatenate([sinks.reshape(N_KV_HEADS, GQA_GROUP), sinks.reshape(N_KV_HEADS, GQA_GROUP)[:, -1:]], axis=1),
            t_new, axis=1)
        sinkcol = jnp.broadcast_to(sink_rows[:, :, None], (N_KV_HEADS, SAMPLE_ROWS, LANES))

        i = l // 2
        if l % 2 == 0:
            wg = ffn_w_gate[i].astype(BF16)
            wu = ffn_w_up[i].astype(BF16)
            wd = ffn_w_down[i].astype(BF16)
            channel_mix = lambda x: _ffn(x, g_ffn, wg, wu, wd, tm=512, tf=1408)
        else:
            wg = moe_w_gate[i].astype(BF16)
            wu = moe_w_up[i].astype(BF16)
            wd = moe_w_down[i].astype(BF16)
            rpad = jnp.pad(moe_router[i], ((0, 0), (0, LANES - N_EXPERTS)))
            rhi, rlo = _split_bf16(rpad)
            channel_mix = lambda x: _moe(x, g_ffn, rhi, rlo, wg, wu, wd)

        u, q, k, v = _project(xp, g_mix, w_in_b, sel, selt, gain, tm=512, q_dtype=BF16)
        xp = _prompt_mix(xp, q, k, v, u, sinks, bias_p, poolw_bd, pscale, w_out_b,
                         batch=batch, seq=seq, tq=512)
        xp = channel_mix(xp)
        pool_p.append(u.reshape(batch, seq, POOL_WIDTH)[:, seq - POOL_BUF:])
        k_p.append(k.reshape(batch, seq, N_KV_HEADS, HEAD_DIM)[:, seq - n_keep:])
        v_p.append(v.reshape(batch, seq, N_KV_HEADS, HEAD_DIM)[:, seq - n_keep:])

        u, q, k, v = _project(xs, g_mix, w_in_b, sel, selt, gain, tm=512, q_dtype=F32)
        xs, ko, vo, po = _sample_mix(
            xs, q, k, v, u,
            state_win_k[l].reshape(n_seq, n_buf, KV_WIDTH),
            state_win_v[l].reshape(n_seq, n_buf, KV_WIDTH),
            state_pool[l], bias_s, sinkcol, poolw_bd, pscale, w_out_b,
            n_seq=n_seq, t_new=t_new, group=16)
        xs = channel_mix(xs)
        pool_s.append(po)
        k_s.append(ko.reshape(n_seq, n_buf, N_KV_HEADS, HEAD_DIM))
        v_s.append(vo.reshape(n_seq, n_buf, N_KV_HEADS, HEAD_DIM))

    return (xp.reshape(batch, seq, D_MODEL), xs.reshape(n_seq, t_new, D_MODEL),
            jnp.stack(pool_p), jnp.stack(k_p), jnp.stack(v_p),
            jnp.stack(pool_s), jnp.stack(k_s), jnp.stack(v_s))
```

```python
import functools
import math

import numpy as np
import jax
import jax.numpy as jnp
from jax import lax
from jax.experimental import pallas as pl
from jax.experimental.pallas import tpu as pltpu

F32 = jnp.float32
BF16 = jnp.bfloat16

D_MODEL = 1024
DEPTH = 2
POOL_WIDTH = 256
POOL_GROUP_DIM = 64
POOL_WINDOWS = (2, 4, 8, 16)
POOL_BUF = 15
HEAD_DIM = 64
N_Q_HEADS = 12
N_KV_HEADS = 4
GQA_GROUP = 3
ATTN_WIDTH = 768
KV_WIDTH = 256
QK_WIDTH = ATTN_WIDTH + KV_WIDTH
IN_WIDTH = 1536
WINDOW = 128
D_FF = 2816
N_EXPERTS = 8
EPS = 1e-6
NEG_INF = float("-inf")

LANES = 128
VMEM_LIMIT = 56 * 1024 * 1024


def _alibi_slopes(n):
    def pow2_slopes(m):
        start = 2.0 ** (-8.0 / m)
        return [start ** (i + 1) for i in range(m)]
    if float(math.log2(n)).is_integer():
        s = pow2_slopes(n)
    else:
        c = 2 ** int(math.floor(math.log2(n)))
        s = pow2_slopes(c) + pow2_slopes(2 * c)[0::2][: n - c]
    return np.array(s, dtype=np.float32)


def _split_bf16(x):
    hi = x.astype(BF16)
    lo = (x - hi.astype(F32)).astype(BF16)
    return hi, lo


def _params(*sem):
    return pltpu.CompilerParams(dimension_semantics=sem, vmem_limit_bytes=VMEM_LIMIT)


def _proj_kernel(x_ref, g_ref, w_ref, gain_ref, u_ref, q_ref, k_ref, v_ref):
    xf = x_ref[...]
    ms = jnp.mean(xf * xf, axis=-1, keepdims=True)
    h = (xf * lax.rsqrt(ms + EPS) * g_ref[...]).astype(BF16)
    proj = jnp.dot(h, w_ref[...], preferred_element_type=F32)
    u_ref[...] = proj[:, :POOL_WIDTH]
    v_ref[...] = proj[:, POOL_WIDTH + QK_WIDTH:]
    lower = lax.broadcasted_iota(jnp.int32, (xf.shape[0], LANES), 1) < HEAD_DIM
    for t in range(QK_WIDTH // LANES):
        c0 = POOL_WIDTH + t * LANES
        x = proj[:, c0:c0 + LANES]
        sq = x * x
        tot = jnp.sum(sq, axis=-1, keepdims=True)
        low = jnp.sum(jnp.where(lower, sq, 0.0), axis=-1, keepdims=True)
        ss = jnp.where(lower, low, tot - low)
        xn = x * lax.rsqrt(ss * (1.0 / HEAD_DIM) + EPS) * gain_ref[:, t * LANES:(t + 1) * LANES]
        if t * LANES < ATTN_WIDTH:
            q_ref[:, t * LANES:(t + 1) * LANES] = xn.astype(q_ref.dtype)
        else:
            k_ref[:, t * LANES - ATTN_WIDTH:(t + 1) * LANES - ATTN_WIDTH] = xn


def _project(x, g, w_in_b, gain, *, tm, q_dtype):
    n = x.shape[0]
    const = lambda i: (0, 0)
    row = lambda i: (i, 0)
    return pl.pallas_call(
        _proj_kernel,
        grid=(n // tm,),
        in_specs=[
            pl.BlockSpec((tm, D_MODEL), row),
            pl.BlockSpec((1, D_MODEL), const),
            pl.BlockSpec((D_MODEL, IN_WIDTH), const),
            pl.BlockSpec((1, QK_WIDTH), const),
        ],
        out_specs=[
            pl.BlockSpec((tm, POOL_WIDTH), row),
            pl.BlockSpec((tm, ATTN_WIDTH), row),
            pl.BlockSpec((tm, KV_WIDTH), row),
            pl.BlockSpec((tm, KV_WIDTH), row),
        ],
        out_shape=[
            jax.ShapeDtypeStruct((n, POOL_WIDTH), F32),
            jax.ShapeDtypeStruct((n, ATTN_WIDTH), q_dtype),
            jax.ShapeDtypeStruct((n, KV_WIDTH), F32),
            jax.ShapeDtypeStruct((n, KV_WIDTH), F32),
        ],
        compiler_params=_params("arbitrary"),
        name="norm_proj",
    )(x, g, w_in_b, gain)


def _pool_window_sums(load_shifted, lane_axis=1):
    x0 = load_shifted(0)
    acc = x0 + load_shifted(1)
    s2 = acc
    for d in range(2, 4):
        acc = acc + load_shifted(d)
    s4 = acc
    for d in range(4, 8):
        acc = acc + load_shifted(d)
    s8 = acc
    for d in range(8, 16):
        acc = acc + load_shifted(d)
    s16 = acc
    lane = lax.broadcasted_iota(jnp.int32, x0.shape, lane_axis)
    win = jnp.where(lane < 64, s2, jnp.where(lane < 128, s4, jnp.where(lane < 192, s8, s16)))
    return x0, win, lane


def _pool_window_len(lane):
    return jnp.where(lane < 64, 2, jnp.where(lane < 128, 4, jnp.where(lane < 192, 8, 16)))


def _prompt_mix_kernel(sink_ref, x_ref, q_ref, kc_ref, kp_ref, vc_ref, vp_ref, uc_ref, up_ref,
                       bias_ref, poolw_ref, pscale_ref, wout_ref,
                       o_ref, kk, vv, uu, mix, *, tq):
    i = pl.program_id(1)
    first = i == 0
    blk = WINDOW
    kk[0:blk] = kp_ref[...].astype(BF16)
    kk[blk:] = kc_ref[...].astype(BF16)
    vv[0:blk] = vp_ref[...].astype(BF16)
    vv[blk:] = vc_ref[...].astype(BF16)
    uu[0:16] = jnp.where(first, 0.0, up_ref[...])
    uu[16:] = uc_ref[...]

    col = lax.broadcasted_iota(jnp.int32, (1, 2 * blk), 1)
    no_prev = jnp.where(jnp.logical_and(first, col < blk), NEG_INF, 0.0)

    for j in range(tq // blk):
        r0 = j * blk
        x0, win, lane = _pool_window_sums(lambda d: uu[16 + r0 - d:16 + r0 - d + blk, :])
        pos = i * tq + r0 + lax.broadcasted_iota(jnp.int32, x0.shape, 0)
        cnt = jnp.minimum(_pool_window_len(lane), pos + 1).astype(F32)
        pooled = (win / cnt - x0).astype(BF16)
        pm = jnp.dot(pooled, poolw_ref[...], preferred_element_type=F32) * pscale_ref[...]
        mix[r0:r0 + blk, 0:POOL_WIDTH] = pm.astype(BF16)
        upper = lax.broadcasted_iota(jnp.int32, (2 * blk, LANES), 1) >= HEAD_DIM
        upper_q = lax.broadcasted_iota(jnp.int32, (blk, LANES), 1) >= HEAD_DIM
        res_even = None
        for g in range(N_KV_HEADS):
            c = (g // 2) * LANES
            kcol = kk[r0:r0 + 2 * blk, c:c + LANES]
            vcol = vv[r0:r0 + 2 * blk, c:c + LANES]
            kswap = jnp.concatenate([kcol[:, HEAD_DIM:], kcol[:, :HEAD_DIM]], axis=1)
            vswap = jnp.concatenate([vcol[:, HEAD_DIM:], vcol[:, :HEAD_DIM]], axis=1)
            k_half, v_half = [], []
            for par in range(2):
                in_half = upper if par == 1 else jnp.logical_not(upper)
                ksrc, vsrc = (kcol, vcol) if par == g % 2 else (kswap, vswap)
                k_half.append(jnp.where(in_half, ksrc, jnp.zeros_like(ksrc)))
                v_half.append(jnp.where(in_half, vsrc, jnp.ones_like(vsrc)))
            for r in range(GQA_GROUP):
                h = GQA_GROUP * g + r
                par = h % 2
                qc = (h // 2) * LANES
                s = lax.dot_general(q_ref[r0:r0 + blk, qc:qc + LANES], k_half[par],
                                    (((1,), (1,)), ((), ())), preferred_element_type=F32)
                s = s + bias_ref[g, r * blk:(r + 1) * blk, :]
                if j == 0:
                    s = s + no_prev
                sink = sink_ref[h]
                m = jnp.maximum(jnp.max(s, axis=-1, keepdims=True), sink)
                p = jnp.exp(s - m).astype(BF16)
                oa = jnp.dot(p, v_half[par], preferred_element_type=F32)
                sum_lane = HEAD_DIM if par == 0 else 0
                res = oa / (oa[:, sum_lane:sum_lane + 1] + jnp.exp(sink - m))
                if par == 0:
                    res_even = res
                else:
                    c0 = POOL_WIDTH + qc
                    mix[r0:r0 + blk, c0:c0 + LANES] = jnp.where(upper_q, res, res_even).astype(BF16)
        o_ref[r0:r0 + blk, :] = x_ref[r0:r0 + blk, :] + jnp.dot(
            mix[r0:r0 + blk, :], wout_ref[...], preferred_element_type=F32)


def _prompt_mix(x, q, k, v, u, sinks, bias, poolw_bd, pscale, wout_b, *, batch, seq, tq):
    nb = seq // tq
    x3 = x.reshape(batch, seq, D_MODEL)
    q3 = q.reshape(batch, seq, ATTN_WIDTH)
    k3 = k.reshape(batch, seq, KV_WIDTH)
    v3 = v.reshape(batch, seq, KV_WIDTH)
    u3 = u.reshape(batch, seq, POOL_WIDTH)
    cur = lambda b, i: (b, i, 0)
    prev_kv = lambda b, i: (b, jnp.maximum(i * (tq // WINDOW) - 1, 0), 0)
    prev_u = lambda b, i: (b, jnp.maximum(i * (tq // 16) - 1, 0), 0)
    const2 = lambda b, i: (0, 0)
    const3 = lambda b, i: (0, 0, 0)
    out = pl.pallas_call(
        functools.partial(_prompt_mix_kernel, tq=tq),
        grid=(batch, nb),
        in_specs=[
            pl.BlockSpec(memory_space=pltpu.SMEM),
            pl.BlockSpec((None, tq, D_MODEL), cur),
            pl.BlockSpec((None, tq, ATTN_WIDTH), cur),
            pl.BlockSpec((None, tq, KV_WIDTH), cur),
            pl.BlockSpec((None, WINDOW, KV_WIDTH), prev_kv),
            pl.BlockSpec((None, tq, KV_WIDTH), cur),
            pl.BlockSpec((None, WINDOW, KV_WIDTH), prev_kv),
            pl.BlockSpec((None, tq, POOL_WIDTH), cur),
            pl.BlockSpec((None, 16, POOL_WIDTH), prev_u),
            pl.BlockSpec((N_KV_HEADS, GQA_GROUP * WINDOW, 2 * WINDOW), const3),
            pl.BlockSpec((POOL_WIDTH, POOL_WIDTH), const2),
            pl.BlockSpec((1, POOL_WIDTH), const2),
            pl.BlockSpec((D_MODEL, D_MODEL), const2),
        ],
        out_specs=pl.BlockSpec((None, tq, D_MODEL), cur),
        out_shape=jax.ShapeDtypeStruct((batch, seq, D_MODEL), F32),
        scratch_shapes=[
            pltpu.VMEM((tq + WINDOW, KV_WIDTH), BF16),
            pltpu.VMEM((tq + WINDOW, KV_WIDTH), BF16),
            pltpu.VMEM((tq + 16, POOL_WIDTH), F32),
            pltpu.VMEM((tq, D_MODEL), BF16),
        ],
        compiler_params=_params("arbitrary", "arbitrary"),
        name="prompt_mix",
    )(sinks, x3, q3, k3, k3, v3, v3, u3, u3, bias, poolw_bd, pscale, wout_b)
    return out.reshape(batch * seq, D_MODEL)


SAMPLE_ROWS = 32
SAMPLE_KEYS = 256


def _sample_mix_kernel(x_ref, q_ref, kn_ref, vn_ref, un_ref, kb_ref, vb_ref, pb_ref,
                       bias_ref, sinkcol_ref, poolw_ref, pscale_ref, wout_ref,
                       o_ref, ko_ref, vo_ref, po_ref,
                       kx, vx, px, mix, *, group, t_new, n_buf):
    rows = group * t_new
    ko_ref[:, 0:n_buf - t_new, :] = kb_ref[:, t_new:n_buf, :]
    ko_ref[:, n_buf - t_new:n_buf, :] = kn_ref[...]
    vo_ref[:, 0:n_buf - t_new, :] = vb_ref[:, t_new:n_buf, :]
    vo_ref[:, n_buf - t_new:n_buf, :] = vn_ref[...]
    pad8 = jnp.zeros((group, 16 - t_new, KV_WIDTH), F32)
    tail = jnp.zeros((group, SAMPLE_KEYS - n_buf - 16, KV_WIDTH), BF16)
    kx[:, 0:n_buf, :] = kb_ref[...].astype(BF16)
    kx[:, n_buf:n_buf + 16, :] = jnp.concatenate([kn_ref[...], pad8], axis=1).astype(BF16)
    kx[:, n_buf + 16:, :] = tail
    vx[:, 0:n_buf, :] = vb_ref[...].astype(BF16)
    vx[:, n_buf:n_buf + 16, :] = jnp.concatenate([vn_ref[...], pad8], axis=1).astype(BF16)
    vx[:, n_buf + 16:, :] = tail

    px[:, 1:16, :] = pb_ref[...]
    px[:, 16:16 + t_new, :] = un_ref[...]
    po_ref[...] = px[:, 16 + t_new - POOL_BUF:16 + t_new, :]
    x0, win, lane = _pool_window_sums(lambda d: px[:, 16 - d:16 - d + t_new, :], lane_axis=2)
    cnt = _pool_window_len(lane).astype(F32)
    pooled = (win / cnt - x0).reshape(rows, POOL_WIDTH).astype(BF16)
    pm = jnp.dot(pooled, poolw_ref[...], preferred_element_type=F32) * pscale_ref[...]
    mix[:, :, 0:POOL_WIDTH] = pm.reshape(group, t_new, POOL_WIDTH)

    q3 = q_ref[...]
    for g in range(N_KV_HEADS):
        heads = [q3[:, :, HEAD_DIM * (GQA_GROUP * g + r):HEAD_DIM * (GQA_GROUP * g + r + 1)]
                 for r in range(GQA_GROUP)]
        qs = jnp.concatenate(heads + [heads[-1]], axis=1).astype(BF16)
        kh = kx[:, :, HEAD_DIM * g:HEAD_DIM * (g + 1)]
        vh = vx[:, :, HEAD_DIM * g:HEAD_DIM * (g + 1)]
        s = jnp.einsum("gqd,gkd->gqk", qs, kh, preferred_element_type=F32) + bias_ref[g][None]
        sink = sinkcol_ref[g][:, 0:1][None]
        m = jnp.maximum(jnp.max(s, axis=-1, keepdims=True), sink)
        p = jnp.exp(s - m)
        den = jnp.sum(p, axis=-1, keepdims=True) + jnp.exp(sink - m)
        o = jnp.einsum("gqk,gkd->gqd", p.astype(BF16), vh, preferred_element_type=F32) / den
        for r in range(GQA_GROUP):
            c0 = POOL_WIDTH + HEAD_DIM * (GQA_GROUP * g + r)
            mix[:, :, c0:c0 + HEAD_DIM] = o[:, r * t_new:(r + 1) * t_new, :]

    mixb = mix[...].reshape(rows, D_MODEL).astype(BF16)
    o_ref[...] = x_ref[...] + jnp.dot(mixb, wout_ref[...], preferred_element_type=F32)


def _sample_mix(x, q, k, v, u, kbuf, vbuf, pbuf, bias, sinkcol, poolw_bd, pscale, wout_b,
                *, n_seq, t_new, group):
    n_buf = kbuf.shape[1]
    q3 = q.reshape(n_seq, t_new, ATTN_WIDTH)
    k3 = k.reshape(n_seq, t_new, KV_WIDTH)
    v3 = v.reshape(n_seq, t_new, KV_WIDTH)
    u3 = u.reshape(n_seq, t_new, POOL_WIDTH)
    rows = group * t_new
    seq3 = lambda i: (i, 0, 0)
    row2 = lambda i: (i, 0)
    const2 = lambda i: (0, 0)
    const3 = lambda i: (0, 0, 0)
    return pl.pallas_call(
        functools.partial(_sample_mix_kernel, group=group, t_new=t_new, n_buf=n_buf),
        grid=(n_seq // group,),
        in_specs=[
            pl.BlockSpec((rows, D_MODEL), row2),
            pl.BlockSpec((group, t_new, ATTN_WIDTH), seq3),
            pl.BlockSpec((group, t_new, KV_WIDTH), seq3),
            pl.BlockSpec((group, t_new, KV_WIDTH), seq3),
            pl.BlockSpec((group, t_new, POOL_WIDTH), seq3),
            pl.BlockSpec((group, n_buf, KV_WIDTH), seq3),
            pl.BlockSpec((group, n_buf, KV_WIDTH), seq3),
            pl.BlockSpec((group, POOL_BUF, POOL_WIDTH), seq3),
            pl.BlockSpec((N_KV_HEADS, SAMPLE_ROWS, SAMPLE_KEYS), const3),
            pl.BlockSpec((N_KV_HEADS, SAMPLE_ROWS, LANES), const3),
            pl.BlockSpec((POOL_WIDTH, POOL_WIDTH), const2),
            pl.BlockSpec((1, POOL_WIDTH), const2),
            pl.BlockSpec((D_MODEL, D_MODEL), const2),
        ],
        out_specs=[
            pl.BlockSpec((rows, D_MODEL), row2),
            pl.BlockSpec((group, n_buf, KV_WIDTH), seq3),
            pl.BlockSpec((group, n_buf, KV_WIDTH), seq3),
            pl.BlockSpec((group, POOL_BUF, POOL_WIDTH), seq3),
        ],
        out_shape=[
            jax.ShapeDtypeStruct((n_seq * t_new, D_MODEL), F32),
            jax.ShapeDtypeStruct((n_seq, n_buf, KV_WIDTH), F32),
            jax.ShapeDtypeStruct((n_seq, n_buf, KV_WIDTH), F32),
            jax.ShapeDtypeStruct((n_seq, POOL_BUF, POOL_WIDTH), F32),
        ],
        scratch_shapes=[
            pltpu.VMEM((group, SAMPLE_KEYS, KV_WIDTH), BF16),
            pltpu.VMEM((group, SAMPLE_KEYS, KV_WIDTH), BF16),
            pltpu.VMEM((group, 16 + t_new, POOL_WIDTH), F32),
            pltpu.VMEM((group, t_new, D_MODEL), F32),
        ],
        compiler_params=_params("arbitrary"),
        name="sample_mix",
    )(x, q3, k3, v3, u3, kbuf, vbuf, pbuf, bias, sinkcol, poolw_bd, pscale, wout_b)


def _rms(xf, g):
    ms = jnp.mean(xf * xf, axis=-1, keepdims=True)
    return xf * lax.rsqrt(ms + EPS) * g


def _swiglu(h, wg, wu, wd):
    gate = jnp.dot(h, wg, preferred_element_type=F32)
    up = jnp.dot(h, wu, preferred_element_type=F32)
    act = (gate * jax.nn.sigmoid(gate) * up).astype(BF16)
    return jnp.dot(act, wd, preferred_element_type=F32)


def _ffn_kernel(x_ref, g_ref, wg_ref, wu_ref, wd_ref, o_ref, hb, acc):
    f = pl.program_id(1)

    @pl.when(f == 0)
    def _():
        hb[...] = _rms(x_ref[...], g_ref[...]).astype(BF16)
        acc[...] = jnp.zeros_like(acc)

    acc[...] += _swiglu(hb[...], wg_ref[...], wu_ref[...], wd_ref[...])

    @pl.when(f == pl.num_programs(1) - 1)
    def _():
        o_ref[...] = x_ref[...] + acc[...]


def _ffn(x, g, wg, wu, wd, *, tm, tf):
    n = x.shape[0]
    dff = wg.shape[1]
    row = lambda i, f: (i, 0)
    const = lambda i, f: (0, 0)
    return pl.pallas_call(
        _ffn_kernel,
        grid=(n // tm, dff // tf),
        in_specs=[
            pl.BlockSpec((tm, D_MODEL), row),
            pl.BlockSpec((1, D_MODEL), const),
            pl.BlockSpec((D_MODEL, tf), lambda i, f: (0, f)),
            pl.BlockSpec((D_MODEL, tf), lambda i, f: (0, f)),
            pl.BlockSpec((tf, D_MODEL), lambda i, f: (f, 0)),
        ],
        out_specs=pl.BlockSpec((tm, D_MODEL), row),
        out_shape=jax.ShapeDtypeStruct((n, D_MODEL), F32),
        scratch_shapes=[
            pltpu.VMEM((tm, D_MODEL), BF16),
            pltpu.VMEM((tm, D_MODEL), F32),
        ],
        compiler_params=_params("arbitrary", "arbitrary"),
        name="dense_ffn",
    )(x, g, wg, wu, wd)


MOE_TOK_TILE = 256
MOE_ROW_TILE = 256
FF_CHUNK = 1408


def _route_kernel(x_ref, g_ref, rcat_ref, rt_ref):
    h = _rms(x_ref[...], g_ref[...])
    hi, lo = _split_bf16(h)
    a = jnp.dot(hi, rcat_ref[...], preferred_element_type=F32)
    lg = (a[:, :LANES] + jnp.dot(lo, rcat_ref[:, :LANES], preferred_element_type=F32)
          + a[:, LANES:])
    lane = lax.broadcasted_iota(jnp.int32, lg.shape, 1)
    lg = jnp.where(lane < N_EXPERTS, lg, NEG_INF)
    m1 = jnp.max(lg, axis=-1, keepdims=True)
    i1 = jnp.min(jnp.where(lg == m1, lane, LANES), axis=-1, keepdims=True)
    lg2 = jnp.where(lane == i1, NEG_INF, lg)
    m2 = jnp.max(lg2, axis=-1, keepdims=True)
    i2 = jnp.min(jnp.where(lg2 == m2, lane, LANES), axis=-1, keepdims=True)
    e2 = jnp.exp(m2 - m1)
    den = 1.0 + e2
    rt_ref[...] = jnp.where(lane == 0, i1.astype(F32),
                            jnp.where(lane == 1, i2.astype(F32),
                                      jnp.where(lane == 2, 1.0 / den,
                                                jnp.where(lane == 3, e2 / den, 0.0))))


def _route(x, g, rcat, *, tm):
    n = x.shape[0]
    row = lambda i: (i, 0)
    const = lambda i: (0, 0)
    return pl.pallas_call(
        _route_kernel,
        grid=(n // tm,),
        in_specs=[
            pl.BlockSpec((tm, D_MODEL), row),
            pl.BlockSpec((1, D_MODEL), const),
            pl.BlockSpec((D_MODEL, 2 * LANES), const),
        ],
        out_specs=pl.BlockSpec((tm, LANES), row),
        out_shape=jax.ShapeDtypeStruct((n, LANES), F32),
        compiler_params=_params("arbitrary"),
        name="moe_route",
    )(x, g, rcat)


def _row_copies_done(hbm_ref, sem, n_rows):
    pltpu.make_async_copy(hbm_ref.at[pl.ds(0, n_rows), :], hbm_ref.at[pl.ds(0, n_rows), :], sem).wait()


def _dispatch_kernel(pos_ref, x_ref, g_ref, xs_zero_ref, xs_ref, hbuf, sem, *, tm):
    del xs_zero_ref
    i = pl.program_id(0)
    slot = lax.rem(i, 2)

    @pl.when(i >= 2)
    def _():
        _row_copies_done(xs_ref, sem.at[slot], 2 * tm)

    hbuf[slot] = _rms(x_ref[...], g_ref[...])

    def issue_from(s):
        def issue(c, carry):
            t0 = pl.multiple_of(c * 8, 8)
            for j in range(8):
                for k in range(2):
                    pltpu.make_async_copy(hbuf.at[s, pl.ds(t0 + j, 1), :],
                                          xs_ref.at[pl.ds(pos_ref[0, c * 16 + 2 * j + k], 1), :],
                                          sem.at[s]).start()
            return carry
        lax.fori_loop(0, tm // 8, issue, 0)

    for s in range(2):
        pl.when(slot == s)(functools.partial(issue_from, s))

    @pl.when(i == pl.num_programs(0) - 1)
    def _():
        _row_copies_done(xs_ref, sem.at[slot], 2 * tm)

        @pl.when(i >= 1)
        def _():
            _row_copies_done(xs_ref, sem.at[1 - slot], 2 * tm)


def _dispatch(x, g, pos, xs_zero, *, tm):
    n = x.shape[0]
    return pl.pallas_call(
        functools.partial(_dispatch_kernel, tm=tm),
        grid=(n // tm,),
        in_specs=[
            pl.BlockSpec((None, 1, 2 * tm), lambda i: (i, 0, 0), memory_space=pltpu.SMEM),
            pl.BlockSpec((tm, D_MODEL), lambda i: (i, 0)),
            pl.BlockSpec((1, D_MODEL), lambda i: (0, 0)),
            pl.BlockSpec(memory_space=pl.ANY),
        ],
        out_specs=pl.BlockSpec(memory_space=pl.ANY),
        out_shape=jax.ShapeDtypeStruct(xs_zero.shape, F32),
        scratch_shapes=[
            pltpu.VMEM((2, tm, D_MODEL), F32),
            pltpu.SemaphoreType.DMA((2,)),
        ],
        input_output_aliases={3: 0},
        compiler_params=_params("arbitrary"),
        name="moe_dispatch",
    )(pos.reshape(n // tm, 1, 2 * tm), x, g, xs_zero)


def _expert_kernel(te_ref, tv_ref, x_ref, wg_ref, wu_ref, wd_ref, y_ref):
    t = pl.program_id(0)

    @pl.when(tv_ref[t] != 0)
    def _():
        h = x_ref[...].astype(BF16)
        acc = None
        for c0 in range(0, D_FF, FF_CHUNK):
            y = _swiglu(h, wg_ref[:, c0:c0 + FF_CHUNK], wu_ref[:, c0:c0 + FF_CHUNK],
                        wd_ref[c0:c0 + FF_CHUNK, :])
            acc = y if acc is None else acc + y
        y_ref[...] = acc

    @pl.when(tv_ref[t] == 0)
    def _():
        y_ref[...] = jnp.zeros_like(y_ref)


def _experts(xs, tile_expert, tile_valid, wg, wu, wd, *, tr):
    rows = xs.shape[0]
    wmap = lambda t, te, tv: (te[t], 0, 0)
    grid_spec = pltpu.PrefetchScalarGridSpec(
        num_scalar_prefetch=2,
        grid=(rows // tr,),
        in_specs=[
            pl.BlockSpec((tr, D_MODEL), lambda t, te, tv: (t, 0)),
            pl.BlockSpec((None, D_MODEL, D_FF), wmap),
            pl.BlockSpec((None, D_MODEL, D_FF), wmap),
            pl.BlockSpec((None, D_FF, D_MODEL), wmap),
        ],
        out_specs=pl.BlockSpec((tr, D_MODEL), lambda t, te, tv: (t, 0)),
    )
    return pl.pallas_call(
        _expert_kernel,
        grid_spec=grid_spec,
        out_shape=jax.ShapeDtypeStruct((rows, D_MODEL), F32),
        compiler_params=_params("arbitrary"),
        name="moe_experts",
    )(tile_expert, tile_valid, xs, wg, wu, wd)


def _combine_kernel(pos_ref, posn_ref, x_ref, rt_ref, ys_ref, o_ref, ybuf, sem, *, tm):
    i = pl.program_id(0)
    last = pl.num_programs(0) - 1
    slot = lax.rem(i, 2)

    def gather(p_ref, s):
        def issue(c, carry):
            t0 = pl.multiple_of(c * 8, 8)
            for j in range(8):
                for k in range(2):
                    pltpu.make_async_copy(ys_ref.at[pl.ds(p_ref[0, c * 16 + 2 * j + k], 1), :],
                                          ybuf.at[s, k, pl.ds(t0 + j, 1), :],
                                          sem.at[s]).start()
            return carry
        lax.fori_loop(0, tm // 8, issue, 0)

    pl.when(i == 0)(functools.partial(gather, pos_ref, 0))
    for s in range(2):
        pl.when(jnp.logical_and(i < last, slot == s))(functools.partial(gather, posn_ref, 1 - s))

    _row_copies_done(ys_ref, sem.at[slot], 2 * tm)
    rt = rt_ref[...]
    o_ref[...] = x_ref[...] + (rt[:, 2:3] * ybuf[slot, 0] + rt[:, 3:4] * ybuf[slot, 1])


def _combine(x, rt, pos, ys, *, tm):
    n = x.shape[0]
    nt = n // tm
    pos3 = pos.reshape(nt, 1, 2 * tm)
    row = lambda i: (i, 0)
    return pl.pallas_call(
        functools.partial(_combine_kernel, tm=tm),
        grid=(nt,),
        in_specs=[
            pl.BlockSpec((None, 1, 2 * tm), lambda i: (i, 0, 0), memory_space=pltpu.SMEM),
            pl.BlockSpec((None, 1, 2 * tm), lambda i: (jnp.minimum(i + 1, nt - 1), 0, 0),
                         memory_space=pltpu.SMEM),
            pl.BlockSpec((tm, D_MODEL), row),
            pl.BlockSpec((tm, LANES), row),
            pl.BlockSpec(memory_space=pl.ANY),
        ],
        out_specs=pl.BlockSpec((tm, D_MODEL), row),
        out_shape=jax.ShapeDtypeStruct((n, D_MODEL), F32),
        scratch_shapes=[
            pltpu.VMEM((2, 2, tm, D_MODEL), F32),
            pltpu.SemaphoreType.DMA((2,)),
        ],
        compiler_params=_params("arbitrary"),
        name="moe_combine",
    )(pos3, pos3, x, rt, ys)


def _moe(x, g, rcat, wg, wu, wd):
    n = x.shape[0]
    tm, tr = MOE_TOK_TILE, MOE_ROW_TILE
    rt = _route(x, g, rcat, tm=tm)
    expert = rt[:, :2].astype(jnp.int32).reshape(2 * n)
    onehot = (expert[:, None] == jnp.arange(N_EXPERTS, dtype=jnp.int32)[None, :]).astype(jnp.int32)
    csum = jnp.cumsum(onehot, axis=0)
    counts = csum[-1]
    group = ((counts + tr - 1) // tr) * tr
    group_end = jnp.cumsum(group)
    pos = jnp.sum(onehot * (csum - 1 + (group_end - group)[None, :]), axis=1).astype(jnp.int32)
    n_rows = 2 * n + N_EXPERTS * tr
    tile_start = jnp.arange(n_rows // tr, dtype=jnp.int32) * tr
    tile_expert = jnp.minimum(jnp.sum(tile_start[:, None] >= group_end[None, :], axis=1),
                              N_EXPERTS - 1).astype(jnp.int32)
    tile_valid = (tile_start < group_end[-1]).astype(jnp.int32)

    xs = _dispatch(x, g, pos, jnp.zeros((n_rows, D_MODEL), F32), tm=tm)
    ys = _experts(xs, tile_expert, tile_valid, wg, wu, wd, tr=tr)
    return _combine(x, rt, pos, ys, tm=tm)


def _prompt_bias():
    slopes = _alibi_slopes(N_Q_HEADS)
    qi = np.arange(WINDOW)[:, None]
    kj = np.arange(2 * WINDOW)[None, :]
    dist = (qi + WINDOW - kj).astype(np.float32)
    valid = (dist >= 0) & (dist < WINDOW)
    out = np.empty((N_KV_HEADS, GQA_GROUP * WINDOW, 2 * WINDOW), np.float32)
    for g in range(N_KV_HEADS):
        for r in range(GQA_GROUP):
            b = np.where(valid, -(slopes[GQA_GROUP * g + r] * dist), -np.inf)
            out[g, r * WINDOW:(r + 1) * WINDOW] = b
    return jnp.asarray(out)


def _sample_bias(t_new, n_buf):
    slopes = _alibi_slopes(N_Q_HEADS)
    t = np.arange(t_new)[:, None]
    j = np.arange(SAMPLE_KEYS)[None, :]
    dist = (t + n_buf - j).astype(np.float32)
    valid = (dist >= 0) & (dist < WINDOW) & (j < n_buf + t_new)
    out = np.empty((N_KV_HEADS, SAMPLE_ROWS, SAMPLE_KEYS), np.float32)
    for g in range(N_KV_HEADS):
        for r in range(GQA_GROUP + 1):
            h = GQA_GROUP * g + min(r, GQA_GROUP - 1)
            out[g, r * t_new:(r + 1) * t_new] = np.where(valid, -(slopes[h] * dist), -np.inf)
    return jnp.asarray(out)


def _block_diag_pool(pool_w_l):
    bd = jnp.zeros((POOL_WIDTH, POOL_WIDTH), F32)
    for g in range(len(POOL_WINDOWS)):
        sl = slice(g * POOL_GROUP_DIM, (g + 1) * POOL_GROUP_DIM)
        bd = bd.at[sl, sl].set(pool_w_l[g])
    return bd.astype(BF16)


def kernel(x_prompt, x_sample, state_pool, state_win_k, state_win_v, norm_mix, w_in, q_norm, k_norm,
           attn_sinks, pool_w, pool_scale, w_out, norm_ffn, ffn_w_gate, ffn_w_up, ffn_w_down,
           moe_router, moe_w_gate, moe_w_up, moe_w_down):
    batch, seq, _ = x_prompt.shape
    n_seq, t_new, _ = x_sample.shape
    n_buf = state_win_k.shape[2]
    n_keep = min(WINDOW, seq)
    assert t_new == 8 and n_buf == WINDOW and SAMPLE_ROWS == (GQA_GROUP + 1) * t_new

    bias_p = _prompt_bias()
    bias_s = _sample_bias(t_new, n_buf)

    xp = x_prompt.reshape(batch * seq, D_MODEL)
    xs = x_sample.reshape(n_seq * t_new, D_MODEL)
    pool_p, k_p, v_p, pool_s, k_s, v_s = [], [], [], [], [], []
    for l in range(DEPTH):
        g_mix = norm_mix[l].reshape(1, D_MODEL)
        g_ffn = norm_ffn[l].reshape(1, D_MODEL)
        w_in_b = w_in[l].astype(BF16)
        w_out_b = w_out[l].astype(BF16)
        gain = jnp.concatenate([jnp.tile(q_norm[l], N_Q_HEADS) * (HEAD_DIM ** -0.5),
                                jnp.tile(k_norm[l], N_KV_HEADS)]).reshape(1, QK_WIDTH)
        poolw_bd = _block_diag_pool(pool_w[l])
        pscale = pool_scale[l].reshape(1, POOL_WIDTH)
        sinks = attn_sinks[l].astype(F32)
        sink_rows = jnp.repeat(
            jnp.concatenate([sinks.reshape(N_KV_HEADS, GQA_GROUP), sinks.reshape(N_KV_HEADS, GQA_GROUP)[:, -1:]], axis=1),
            t_new, axis=1)
        sinkcol = jnp.broadcast_to(sink_rows[:, :, None], (N_KV_HEADS, SAMPLE_ROWS, LANES))

        i = l // 2
        if l % 2 == 0:
            wg = ffn_w_gate[i].astype(BF16)
            wu = ffn_w_up[i].astype(BF16)
            wd = ffn_w_down[i].astype(BF16)
            channel_mix = lambda x: _ffn(x, g_ffn, wg, wu, wd, tm=512, tf=1408)
        else:
            wg = moe_w_gate[i].astype(BF16)
            wu = moe_w_up[i].astype(BF16)
            wd = moe_w_down[i].astype(BF16)
            rpad = jnp.pad(moe_router[i], ((0, 0), (0, LANES - N_EXPERTS)))
            rcat = jnp.concatenate(_split_bf16(rpad), axis=1)
            channel_mix = lambda x: _moe(x, g_ffn, rcat, wg, wu, wd)

        u, q, k, v = _project(xp, g_mix, w_in_b, gain, tm=512, q_dtype=BF16)
        xp = _prompt_mix(xp, q, k, v, u, sinks, bias_p, poolw_bd, pscale, w_out_b,
                         batch=batch, seq=seq, tq=512)
        xp = channel_mix(xp)
        pool_p.append(u.reshape(batch, seq, POOL_WIDTH)[:, seq - POOL_BUF:])
        k_p.append(k.reshape(batch, seq, N_KV_HEADS, HEAD_DIM)[:, seq - n_keep:])
        v_p.append(v.reshape(batch, seq, N_KV_HEADS, HEAD_DIM)[:, seq - n_keep:])

        u, q, k, v = _project(xs, g_mix, w_in_b, gain, tm=512, q_dtype=F32)
        xs, ko, vo, po = _sample_mix(
            xs, q, k, v, u,
            state_win_k[l].reshape(n_seq, n_buf, KV_WIDTH),
            state_win_v[l].reshape(n_seq, n_buf, KV_WIDTH),
            state_pool[l], bias_s, sinkcol, poolw_bd, pscale, w_out_b,
            n_seq=n_seq, t_new=t_new, group=16)
        xs = channel_mix(xs)
        pool_s.append(po)
        k_s.append(ko.reshape(n_seq, n_buf, N_KV_HEADS, HEAD_DIM))
        v_s.append(vo.reshape(n_seq, n_buf, N_KV_HEADS, HEAD_DIM))

    return (xp.reshape(batch, seq, D_MODEL), xs.reshape(n_seq, t_new, D_MODEL),
            jnp.stack(pool_p), jnp.stack(k_p), jnp.stack(v_p),
            jnp.stack(pool_s), jnp.stack(k_s), jnp.stack(v_s))
```

```python
import functools
import math

import numpy as np
import jax
import jax.numpy as jnp
from jax import lax
from jax.experimental import pallas as pl
from jax.experimental.pallas import tpu as pltpu

F32 = jnp.float32
BF16 = jnp.bfloat16

D_MODEL = 1024
DEPTH = 2
POOL_WIDTH = 256
POOL_GROUP_DIM = 64
POOL_WINDOWS = (2, 4, 8, 16)
POOL_BUF = 15
HEAD_DIM = 64
N_Q_HEADS = 12
N_KV_HEADS = 4
GQA_GROUP = 3
ATTN_WIDTH = 768
KV_WIDTH = 256
QK_WIDTH = ATTN_WIDTH + KV_WIDTH
IN_WIDTH = 1536
WINDOW = 128
D_FF = 2816
N_EXPERTS = 8
EPS = 1e-6
NEG_INF = float("-inf")

LANES = 128
VMEM_LIMIT = 56 * 1024 * 1024


def _alibi_slopes(n):
    def pow2_slopes(m):
        start = 2.0 ** (-8.0 / m)
        return [start ** (i + 1) for i in range(m)]
    if float(math.log2(n)).is_integer():
        s = pow2_slopes(n)
    else:
        c = 2 ** int(math.floor(math.log2(n)))
        s = pow2_slopes(c) + pow2_slopes(2 * c)[0::2][: n - c]
    return np.array(s, dtype=np.float32)


def _split_bf16(x):
    hi = x.astype(BF16)
    lo = (x - hi.astype(F32)).astype(BF16)
    return hi, lo


def _params(*sem):
    return pltpu.CompilerParams(dimension_semantics=sem, vmem_limit_bytes=VMEM_LIMIT)


def _proj_kernel(x_ref, g_ref, w_ref, gain_ref, u_ref, q_ref, k_ref, v_ref):
    xf = x_ref[...]
    ms = jnp.mean(xf * xf, axis=-1, keepdims=True)
    h = (xf * lax.rsqrt(ms + EPS) * g_ref[...]).astype(BF16)
    proj = jnp.dot(h, w_ref[...], preferred_element_type=F32)
    u_ref[...] = proj[:, :POOL_WIDTH]
    v_ref[...] = proj[:, POOL_WIDTH + QK_WIDTH:]
    lower = lax.broadcasted_iota(jnp.int32, (xf.shape[0], LANES), 1) < HEAD_DIM
    for t in range(QK_WIDTH // LANES):
        c0 = POOL_WIDTH + t * LANES
        x = proj[:, c0:c0 + LANES]
        sq = x * x
        tot = jnp.sum(sq, axis=-1, keepdims=True)
        low = jnp.sum(jnp.where(lower, sq, 0.0), axis=-1, keepdims=True)
        ss = jnp.where(lower, low, tot - low)
        xn = x * lax.rsqrt(ss * (1.0 / HEAD_DIM) + EPS) * gain_ref[:, t * LANES:(t + 1) * LANES]
        if t * LANES < ATTN_WIDTH:
            q_ref[:, t * LANES:(t + 1) * LANES] = xn.astype(q_ref.dtype)
        else:
            k_ref[:, t * LANES - ATTN_WIDTH:(t + 1) * LANES - ATTN_WIDTH] = xn


def _project(x, g, w_in_b, gain, *, tm, q_dtype):
    n = x.shape[0]
    const = lambda i: (0, 0)
    row = lambda i: (i, 0)
    return pl.pallas_call(
        _proj_kernel,
        grid=(n // tm,),
        in_specs=[
            pl.BlockSpec((tm, D_MODEL), row),
            pl.BlockSpec((1, D_MODEL), const),
            pl.BlockSpec((D_MODEL, IN_WIDTH), const),
            pl.BlockSpec((1, QK_WIDTH), const),
        ],
        out_specs=[
            pl.BlockSpec((tm, POOL_WIDTH), row),
            pl.BlockSpec((tm, ATTN_WIDTH), row),
            pl.BlockSpec((tm, KV_WIDTH), row),
            pl.BlockSpec((tm, KV_WIDTH), row),
        ],
        out_shape=[
            jax.ShapeDtypeStruct((n, POOL_WIDTH), F32),
            jax.ShapeDtypeStruct((n, ATTN_WIDTH), q_dtype),
            jax.ShapeDtypeStruct((n, KV_WIDTH), F32),
            jax.ShapeDtypeStruct((n, KV_WIDTH), F32),
        ],
        compiler_params=_params("arbitrary"),
        name="norm_proj",
    )(x, g, w_in_b, gain)


def _pool_window_sums(load_shifted, lane_axis=1):
    x0 = load_shifted(0)
    acc = x0 + load_shifted(1)
    s2 = acc
    for d in range(2, 4):
        acc = acc + load_shifted(d)
    s4 = acc
    for d in range(4, 8):
        acc = acc + load_shifted(d)
    s8 = acc
    for d in range(8, 16):
        acc = acc + load_shifted(d)
    s16 = acc
    lane = lax.broadcasted_iota(jnp.int32, x0.shape, lane_axis)
    win = jnp.where(lane < 64, s2, jnp.where(lane < 128, s4, jnp.where(lane < 192, s8, s16)))
    return x0, win, lane


def _pool_window_len(lane):
    return jnp.where(lane < 64, 2, jnp.where(lane < 128, 4, jnp.where(lane < 192, 8, 16)))


def _prompt_mix_kernel(sink_ref, x_ref, q_ref, kc_ref, kp_ref, vc_ref, vp_ref, uc_ref, up_ref,
                       bias_ref, poolw_ref, pscale_ref, wout_ref,
                       o_ref, kk, vv, uu, mix, *, tq):
    i = pl.program_id(1)
    first = i == 0
    blk = WINDOW
    kk[0:blk] = kp_ref[...].astype(BF16)
    kk[blk:] = kc_ref[...].astype(BF16)
    vv[0:blk] = vp_ref[...].astype(BF16)
    vv[blk:] = vc_ref[...].astype(BF16)
    uu[0:16] = jnp.where(first, 0.0, up_ref[...])
    uu[16:] = uc_ref[...]

    col = lax.broadcasted_iota(jnp.int32, (1, 2 * blk), 1)
    no_prev = jnp.where(jnp.logical_and(first, col < blk), NEG_INF, 0.0)

    for j in range(tq // blk):
        r0 = j * blk
        x0, win, lane = _pool_window_sums(lambda d: uu[16 + r0 - d:16 + r0 - d + blk, :])
        pos = i * tq + r0 + lax.broadcasted_iota(jnp.int32, x0.shape, 0)
        cnt = jnp.minimum(_pool_window_len(lane), pos + 1).astype(F32)
        pooled = (win / cnt - x0).astype(BF16)
        pm = jnp.dot(pooled, poolw_ref[...], preferred_element_type=F32) * pscale_ref[...]
        mix[r0:r0 + blk, 0:POOL_WIDTH] = pm.astype(BF16)
        upper = lax.broadcasted_iota(jnp.int32, (2 * blk, LANES), 1) >= HEAD_DIM
        upper_q = lax.broadcasted_iota(jnp.int32, (blk, LANES), 1) >= HEAD_DIM
        res_even = None
        for g in range(N_KV_HEADS):
            c = (g // 2) * LANES
            kcol = kk[r0:r0 + 2 * blk, c:c + LANES]
            vcol = vv[r0:r0 + 2 * blk, c:c + LANES]
            kswap = jnp.concatenate([kcol[:, HEAD_DIM:], kcol[:, :HEAD_DIM]], axis=1)
            vswap = jnp.concatenate([vcol[:, HEAD_DIM:], vcol[:, :HEAD_DIM]], axis=1)
            k_half, v_half = [], []
            for par in range(2):
                in_half = upper if par == 1 else jnp.logical_not(upper)
                ksrc, vsrc = (kcol, vcol) if par == g % 2 else (kswap, vswap)
                k_half.append(jnp.where(in_half, ksrc, jnp.zeros_like(ksrc)))
                v_half.append(jnp.where(in_half, vsrc, jnp.ones_like(vsrc)))
            for r in range(GQA_GROUP):
                h = GQA_GROUP * g + r
                par = h % 2
                qc = (h // 2) * LANES
                s = lax.dot_general(q_ref[r0:r0 + blk, qc:qc + LANES], k_half[par],
                                    (((1,), (1,)), ((), ())), preferred_element_type=F32)
                s = s + bias_ref[g, r * blk:(r + 1) * blk, :]
                if j == 0:
                    s = s + no_prev
                sink = sink_ref[h]
                m = jnp.maximum(jnp.max(s, axis=-1, keepdims=True), sink)
                p = jnp.exp(s - m).astype(BF16)
                oa = jnp.dot(p, v_half[par], preferred_element_type=F32)
                sum_lane = HEAD_DIM if par == 0 else 0
                res = oa / (oa[:, sum_lane:sum_lane + 1] + jnp.exp(sink - m))
                if par == 0:
                    res_even = res
                else:
                    c0 = POOL_WIDTH + qc
                    mix[r0:r0 + blk, c0:c0 + LANES] = jnp.where(upper_q, res, res_even).astype(BF16)
        o_ref[r0:r0 + blk, :] = x_ref[r0:r0 + blk, :] + jnp.dot(
            mix[r0:r0 + blk, :], wout_ref[...], preferred_element_type=F32)


def _prompt_mix(x, q, k, v, u, sinks, bias, poolw_bd, pscale, wout_b, *, batch, seq, tq):
    nb = seq // tq
    x3 = x.reshape(batch, seq, D_MODEL)
    q3 = q.reshape(batch, seq, ATTN_WIDTH)
    k3 = k.reshape(batch, seq, KV_WIDTH)
    v3 = v.reshape(batch, seq, KV_WIDTH)
    u3 = u.reshape(batch, seq, POOL_WIDTH)
    cur = lambda b, i: (b, i, 0)
    prev_kv = lambda b, i: (b, jnp.maximum(i * (tq // WINDOW) - 1, 0), 0)
    prev_u = lambda b, i: (b, jnp.maximum(i * (tq // 16) - 1, 0), 0)
    const2 = lambda b, i: (0, 0)
    const3 = lambda b, i: (0, 0, 0)
    out = pl.pallas_call(
        functools.partial(_prompt_mix_kernel, tq=tq),
        grid=(batch, nb),
        in_specs=[
            pl.BlockSpec(memory_space=pltpu.SMEM),
            pl.BlockSpec((None, tq, D_MODEL), cur),
            pl.BlockSpec((None, tq, ATTN_WIDTH), cur),
            pl.BlockSpec((None, tq, KV_WIDTH), cur),
            pl.BlockSpec((None, WINDOW, KV_WIDTH), prev_kv),
            pl.BlockSpec((None, tq, KV_WIDTH), cur),
            pl.BlockSpec((None, WINDOW, KV_WIDTH), prev_kv),
            pl.BlockSpec((None, tq, POOL_WIDTH), cur),
            pl.BlockSpec((None, 16, POOL_WIDTH), prev_u),
            pl.BlockSpec((N_KV_HEADS, GQA_GROUP * WINDOW, 2 * WINDOW), const3),
            pl.BlockSpec((POOL_WIDTH, POOL_WIDTH), const2),
            pl.BlockSpec((1, POOL_WIDTH), const2),
            pl.BlockSpec((D_MODEL, D_MODEL), const2),
        ],
        out_specs=pl.BlockSpec((None, tq, D_MODEL), cur),
        out_shape=jax.ShapeDtypeStruct((batch, seq, D_MODEL), F32),
        scratch_shapes=[
            pltpu.VMEM((tq + WINDOW, KV_WIDTH), BF16),
            pltpu.VMEM((tq + WINDOW, KV_WIDTH), BF16),
            pltpu.VMEM((tq + 16, POOL_WIDTH), F32),
            pltpu.VMEM((tq, D_MODEL), BF16),
        ],
        compiler_params=_params("arbitrary", "arbitrary"),
        name="prompt_mix",
    )(sinks, x3, q3, k3, k3, v3, v3, u3, u3, bias, poolw_bd, pscale, wout_b)
    return out.reshape(batch * seq, D_MODEL)


SAMPLE_ROWS = 32
SAMPLE_KEYS = 256


def _sample_mix_kernel(x_ref, q_ref, kn_ref, vn_ref, un_ref, kb_ref, vb_ref, pb_ref,
                       bias_ref, sinkcol_ref, poolw_ref, pscale_ref, wout_ref,
                       o_ref, ko_ref, vo_ref, po_ref,
                       kx, vx, px, mix, *, group, t_new, n_buf):
    rows = group * t_new
    ko_ref[:, 0:n_buf - t_new, :] = kb_ref[:, t_new:n_buf, :]
    ko_ref[:, n_buf - t_new:n_buf, :] = kn_ref[...]
    vo_ref[:, 0:n_buf - t_new, :] = vb_ref[:, t_new:n_buf, :]
    vo_ref[:, n_buf - t_new:n_buf, :] = vn_ref[...]
    pad8 = jnp.zeros((group, 16 - t_new, KV_WIDTH), F32)
    tail = jnp.zeros((group, SAMPLE_KEYS - n_buf - 16, KV_WIDTH), BF16)
    kx[:, 0:n_buf, :] = kb_ref[...].astype(BF16)
    kx[:, n_buf:n_buf + 16, :] = jnp.concatenate([kn_ref[...], pad8], axis=1).astype(BF16)
    kx[:, n_buf + 16:, :] = tail
    vx[:, 0:n_buf, :] = vb_ref[...].astype(BF16)
    vx[:, n_buf:n_buf + 16, :] = jnp.concatenate([vn_ref[...], pad8], axis=1).astype(BF16)
    vx[:, n_buf + 16:, :] = tail

    px[:, 1:16, :] = pb_ref[...]
    px[:, 16:16 + t_new, :] = un_ref[...]
    po_ref[...] = px[:, 16 + t_new - POOL_BUF:16 + t_new, :]
    x0, win, lane = _pool_window_sums(lambda d: px[:, 16 - d:16 - d + t_new, :], lane_axis=2)
    cnt = _pool_window_len(lane).astype(F32)
    pooled = (win / cnt - x0).reshape(rows, POOL_WIDTH).astype(BF16)
    pm = jnp.dot(pooled, poolw_ref[...], preferred_element_type=F32) * pscale_ref[...]
    mix[:, :, 0:POOL_WIDTH] = pm.reshape(group, t_new, POOL_WIDTH)

    q3 = q_ref[...]
    for g in range(N_KV_HEADS):
        heads = [q3[:, :, HEAD_DIM * (GQA_GROUP * g + r):HEAD_DIM * (GQA_GROUP * g + r + 1)]
                 for r in range(GQA_GROUP)]
        qs = jnp.concatenate(heads + [heads[-1]], axis=1).astype(BF16)
        kh = kx[:, :, HEAD_DIM * g:HEAD_DIM * (g + 1)]
        vh = vx[:, :, HEAD_DIM * g:HEAD_DIM * (g + 1)]
        s = jnp.einsum("gqd,gkd->gqk", qs, kh, preferred_element_type=F32) + bias_ref[g][None]
        sink = sinkcol_ref[g][:, 0:1][None]
        m = jnp.maximum(jnp.max(s, axis=-1, keepdims=True), sink)
        p = jnp.exp(s - m)
        den = jnp.sum(p, axis=-1, keepdims=True) + jnp.exp(sink - m)
        o = jnp.einsum("gqk,gkd->gqd", p.astype(BF16), vh, preferred_element_type=F32) / den
        for r in range(GQA_GROUP):
            c0 = POOL_WIDTH + HEAD_DIM * (GQA_GROUP * g + r)
            mix[:, :, c0:c0 + HEAD_DIM] = o[:, r * t_new:(r + 1) * t_new, :]

    mixb = mix[...].reshape(rows, D_MODEL).astype(BF16)
    o_ref[...] = x_ref[...] + jnp.dot(mixb, wout_ref[...], preferred_element_type=F32)


def _sample_mix(x, q, k, v, u, kbuf, vbuf, pbuf, bias, sinkcol, poolw_bd, pscale, wout_b,
                *, n_seq, t_new, group):
    n_buf = kbuf.shape[1]
    q3 = q.reshape(n_seq, t_new, ATTN_WIDTH)
    k3 = k.reshape(n_seq, t_new, KV_WIDTH)
    v3 = v.reshape(n_seq, t_new, KV_WIDTH)
    u3 = u.reshape(n_seq, t_new, POOL_WIDTH)
    rows = group * t_new
    seq3 = lambda i: (i, 0, 0)
    row2 = lambda i: (i, 0)
    const2 = lambda i: (0, 0)
    const3 = lambda i: (0, 0, 0)
    return pl.pallas_call(
        functools.partial(_sample_mix_kernel, group=group, t_new=t_new, n_buf=n_buf),
        grid=(n_seq // group,),
        in_specs=[
            pl.BlockSpec((rows, D_MODEL), row2),
            pl.BlockSpec((group, t_new, ATTN_WIDTH), seq3),
            pl.BlockSpec((group, t_new, KV_WIDTH), seq3),
            pl.BlockSpec((group, t_new, KV_WIDTH), seq3),
            pl.BlockSpec((group, t_new, POOL_WIDTH), seq3),
            pl.BlockSpec((group, n_buf, KV_WIDTH), seq3),
            pl.BlockSpec((group, n_buf, KV_WIDTH), seq3),
            pl.BlockSpec((group, POOL_BUF, POOL_WIDTH), seq3),
            pl.BlockSpec((N_KV_HEADS, SAMPLE_ROWS, SAMPLE_KEYS), const3),
            pl.BlockSpec((N_KV_HEADS, SAMPLE_ROWS, LANES), const3),
            pl.BlockSpec((POOL_WIDTH, POOL_WIDTH), const2),
            pl.BlockSpec((1, POOL_WIDTH), const2),
            pl.BlockSpec((D_MODEL, D_MODEL), const2),
        ],
        out_specs=[
            pl.BlockSpec((rows, D_MODEL), row2),
            pl.BlockSpec((group, n_buf, KV_WIDTH), seq3),
            pl.BlockSpec((group, n_buf, KV_WIDTH), seq3),
            pl.BlockSpec((group, POOL_BUF, POOL_WIDTH), seq3),
        ],
        out_shape=[
            jax.ShapeDtypeStruct((n_seq * t_new, D_MODEL), F32),
            jax.ShapeDtypeStruct((n_seq, n_buf, KV_WIDTH), F32),
            jax.ShapeDtypeStruct((n_seq, n_buf, KV_WIDTH), F32),
            jax.ShapeDtypeStruct((n_seq, POOL_BUF, POOL_WIDTH), F32),
        ],
        scratch_shapes=[
            pltpu.VMEM((group, SAMPLE_KEYS, KV_WIDTH), BF16),
            pltpu.VMEM((group, SAMPLE_KEYS, KV_WIDTH), BF16),
            pltpu.VMEM((group, 16 + t_new, POOL_WIDTH), F32),
            pltpu.VMEM((group, t_new, D_MODEL), F32),
        ],
        compiler_params=_params("arbitrary"),
        name="sample_mix",
    )(x, q3, k3, v3, u3, kbuf, vbuf, pbuf, bias, sinkcol, poolw_bd, pscale, wout_b)


def _rms(xf, g):
    ms = jnp.mean(xf * xf, axis=-1, keepdims=True)
    return xf * lax.rsqrt(ms + EPS) * g


def _swiglu(h, wg, wu, wd):
    gate = jnp.dot(h, wg, preferred_element_type=F32)
    up = jnp.dot(h, wu, preferred_element_type=F32)
    act = (gate * jax.nn.sigmoid(gate) * up).astype(BF16)
    return jnp.dot(act, wd, preferred_element_type=F32)


def _swiglu_chunked(h, wg_ref, wu_ref, wd_ref):
    acc = None
    for c0 in range(0, D_FF, FF_CHUNK):
        y = _swiglu(h, wg_ref[:, c0:c0 + FF_CHUNK], wu_ref[:, c0:c0 + FF_CHUNK],
                    wd_ref[c0:c0 + FF_CHUNK, :])
        acc = y if acc is None else acc + y
    return acc


def _ffn_kernel(x_ref, g_ref, wg_ref, wu_ref, wd_ref, o_ref):
    xf = x_ref[...]
    h = _rms(xf, g_ref[...]).astype(BF16)
    o_ref[...] = xf + _swiglu_chunked(h, wg_ref, wu_ref, wd_ref)


def _ffn(x, g, wg, wu, wd, *, tm):
    n = x.shape[0]
    row = lambda i: (i, 0)
    const = lambda i: (0, 0)
    return pl.pallas_call(
        _ffn_kernel,
        grid=(n // tm,),
        in_specs=[
            pl.BlockSpec((tm, D_MODEL), row),
            pl.BlockSpec((1, D_MODEL), const),
            pl.BlockSpec((D_MODEL, D_FF), const),
            pl.BlockSpec((D_MODEL, D_FF), const),
            pl.BlockSpec((D_FF, D_MODEL), const),
        ],
        out_specs=pl.BlockSpec((tm, D_MODEL), row),
        out_shape=jax.ShapeDtypeStruct((n, D_MODEL), F32),
        compiler_params=_params("arbitrary"),
        name="dense_ffn",
    )(x, g, wg, wu, wd)


MOE_TOK_TILE = 256
MOE_ROW_TILE = 256
FF_CHUNK = 1408


def _route_kernel(x_ref, g_ref, rcat_ref, rt_ref):
    h = _rms(x_ref[...], g_ref[...])
    hi, lo = _split_bf16(h)
    a = jnp.dot(hi, rcat_ref[...], preferred_element_type=F32)
    lg = (a[:, :LANES] + jnp.dot(lo, rcat_ref[:, :LANES], preferred_element_type=F32)
          + a[:, LANES:])
    lane = lax.broadcasted_iota(jnp.int32, lg.shape, 1)
    lg = jnp.where(lane < N_EXPERTS, lg, NEG_INF)
    m1 = jnp.max(lg, axis=-1, keepdims=True)
    i1 = jnp.min(jnp.where(lg == m1, lane, LANES), axis=-1, keepdims=True)
    lg2 = jnp.where(lane == i1, NEG_INF, lg)
    m2 = jnp.max(lg2, axis=-1, keepdims=True)
    i2 = jnp.min(jnp.where(lg2 == m2, lane, LANES), axis=-1, keepdims=True)
    e2 = jnp.exp(m2 - m1)
    den = 1.0 + e2
    rt_ref[...] = jnp.where(lane == 0, i1.astype(F32),
                            jnp.where(lane == 1, i2.astype(F32),
                                      jnp.where(lane == 2, 1.0 / den,
                                                jnp.where(lane == 3, e2 / den, 0.0))))


def _route(x, g, rcat, *, tm):
    n = x.shape[0]
    row = lambda i: (i, 0)
    const = lambda i: (0, 0)
    return pl.pallas_call(
        _route_kernel,
        grid=(n // tm,),
        in_specs=[
            pl.BlockSpec((tm, D_MODEL), row),
            pl.BlockSpec((1, D_MODEL), const),
            pl.BlockSpec((D_MODEL, 2 * LANES), const),
        ],
        out_specs=pl.BlockSpec((tm, LANES), row),
        out_shape=jax.ShapeDtypeStruct((n, LANES), F32),
        compiler_params=_params("arbitrary"),
        name="moe_route",
    )(x, g, rcat)


def _row_copies_done(hbm_ref, sem, n_rows):
    pltpu.make_async_copy(hbm_ref.at[pl.ds(0, n_rows), :], hbm_ref.at[pl.ds(0, n_rows), :], sem).wait()


def _dispatch_kernel(pos_ref, x_ref, g_ref, xs_zero_ref, xs_ref, hbuf, sem, *, tm):
    del xs_zero_ref
    i = pl.program_id(0)
    slot = lax.rem(i, 2)

    @pl.when(i >= 2)
    def _():
        _row_copies_done(xs_ref, sem.at[slot], 2 * tm)

    hbuf[slot] = _rms(x_ref[...], g_ref[...])

    def issue_from(s):
        def issue(c, carry):
            t0 = pl.multiple_of(c * 8, 8)
            for j in range(8):
                for k in range(2):
                    pltpu.make_async_copy(hbuf.at[s, pl.ds(t0 + j, 1), :],
                                          xs_ref.at[pl.ds(pos_ref[0, c * 16 + 2 * j + k], 1), :],
                                          sem.at[s]).start()
            return carry
        lax.fori_loop(0, tm // 8, issue, 0)

    for s in range(2):
        pl.when(slot == s)(functools.partial(issue_from, s))

    @pl.when(i == pl.num_programs(0) - 1)
    def _():
        _row_copies_done(xs_ref, sem.at[slot], 2 * tm)

        @pl.when(i >= 1)
        def _():
            _row_copies_done(xs_ref, sem.at[1 - slot], 2 * tm)


def _dispatch(x, g, pos, xs_zero, *, tm):
    n = x.shape[0]
    return pl.pallas_call(
        functools.partial(_dispatch_kernel, tm=tm),
        grid=(n // tm,),
        in_specs=[
            pl.BlockSpec((None, 1, 2 * tm), lambda i: (i, 0, 0), memory_space=pltpu.SMEM),
            pl.BlockSpec((tm, D_MODEL), lambda i: (i, 0)),
            pl.BlockSpec((1, D_MODEL), lambda i: (0, 0)),
            pl.BlockSpec(memory_space=pl.ANY),
        ],
        out_specs=pl.BlockSpec(memory_space=pl.ANY),
        out_shape=jax.ShapeDtypeStruct(xs_zero.shape, F32),
        scratch_shapes=[
            pltpu.VMEM((2, tm, D_MODEL), F32),
            pltpu.SemaphoreType.DMA((2,)),
        ],
        input_output_aliases={3: 0},
        compiler_params=_params("arbitrary"),
        name="moe_dispatch",
    )(pos.reshape(n // tm, 1, 2 * tm), x, g, xs_zero)


def _expert_kernel(te_ref, tv_ref, x_ref, wg_ref, wu_ref, wd_ref, y_ref):
    t = pl.program_id(0)

    @pl.when(tv_ref[t] != 0)
    def _():
        y_ref[...] = _swiglu_chunked(x_ref[...].astype(BF16), wg_ref, wu_ref, wd_ref)

    @pl.when(tv_ref[t] == 0)
    def _():
        y_ref[...] = jnp.zeros_like(y_ref)


def _experts(xs, tile_expert, tile_valid, wg, wu, wd, *, tr):
    rows = xs.shape[0]
    wmap = lambda t, te, tv: (te[t], 0, 0)
    grid_spec = pltpu.PrefetchScalarGridSpec(
        num_scalar_prefetch=2,
        grid=(rows // tr,),
        in_specs=[
            pl.BlockSpec((tr, D_MODEL), lambda t, te, tv: (t, 0)),
            pl.BlockSpec((None, D_MODEL, D_FF), wmap),
            pl.BlockSpec((None, D_MODEL, D_FF), wmap),
            pl.BlockSpec((None, D_FF, D_MODEL), wmap),
        ],
        out_specs=pl.BlockSpec((tr, D_MODEL), lambda t, te, tv: (t, 0)),
    )
    return pl.pallas_call(
        _expert_kernel,
        grid_spec=grid_spec,
        out_shape=jax.ShapeDtypeStruct((rows, D_MODEL), F32),
        compiler_params=_params("arbitrary"),
        name="moe_experts",
    )(tile_expert, tile_valid, xs, wg, wu, wd)


def _combine_kernel(pos_ref, posn_ref, x_ref, rt_ref, ys_ref, o_ref, ybuf, sem, *, tm):
    i = pl.program_id(0)
    last = pl.num_programs(0) - 1
    slot = lax.rem(i, 2)

    def gather(p_ref, s):
        def issue(c, carry):
            t0 = pl.multiple_of(c * 8, 8)
            for j in range(8):
                for k in range(2):
                    pltpu.make_async_copy(ys_ref.at[pl.ds(p_ref[0, c * 16 + 2 * j + k], 1), :],
                                          ybuf.at[s, k, pl.ds(t0 + j, 1), :],
                                          sem.at[s]).start()
            return carry
        lax.fori_loop(0, tm // 8, issue, 0)

    pl.when(i == 0)(functools.partial(gather, pos_ref, 0))
    for s in range(2):
        pl.when(jnp.logical_and(i < last, slot == s))(functools.partial(gather, posn_ref, 1 - s))

    _row_copies_done(ys_ref, sem.at[slot], 2 * tm)
    rt = rt_ref[...]
    o_ref[...] = x_ref[...] + (rt[:, 2:3] * ybuf[slot, 0] + rt[:, 3:4] * ybuf[slot, 1])


def _combine(x, rt, pos, ys, *, tm):
    n = x.shape[0]
    nt = n // tm
    pos3 = pos.reshape(nt, 1, 2 * tm)
    row = lambda i: (i, 0)
    return pl.pallas_call(
        functools.partial(_combine_kernel, tm=tm),
        grid=(nt,),
        in_specs=[
            pl.BlockSpec((None, 1, 2 * tm), lambda i: (i, 0, 0), memory_space=pltpu.SMEM),
            pl.BlockSpec((None, 1, 2 * tm), lambda i: (jnp.minimum(i + 1, nt - 1), 0, 0),
                         memory_space=pltpu.SMEM),
            pl.BlockSpec((tm, D_MODEL), row),
            pl.BlockSpec((tm, LANES), row),
            pl.BlockSpec(memory_space=pl.ANY),
        ],
        out_specs=pl.BlockSpec((tm, D_MODEL), row),
        out_shape=jax.ShapeDtypeStruct((n, D_MODEL), F32),
        scratch_shapes=[
            pltpu.VMEM((2, 2, tm, D_MODEL), F32),
            pltpu.SemaphoreType.DMA((2,)),
        ],
        compiler_params=_params("arbitrary"),
        name="moe_combine",
    )(pos3, pos3, x, rt, ys)


def _moe(x, g, rcat, wg, wu, wd):
    n = x.shape[0]
    tm, tr = MOE_TOK_TILE, MOE_ROW_TILE
    rt = _route(x, g, rcat, tm=tm)
    expert = rt[:, :2].astype(jnp.int32).reshape(2 * n)
    onehot = (expert[:, None] == jnp.arange(N_EXPERTS, dtype=jnp.int32)[None, :]).astype(jnp.int32)
    csum = jnp.cumsum(onehot, axis=0)
    counts = csum[-1]
    group = ((counts + tr - 1) // tr) * tr
    group_end = jnp.cumsum(group)
    pos = jnp.sum(onehot * (csum - 1 + (group_end - group)[None, :]), axis=1).astype(jnp.int32)
    n_rows = 2 * n + N_EXPERTS * tr
    tile_start = jnp.arange(n_rows // tr, dtype=jnp.int32) * tr
    tile_expert = jnp.minimum(jnp.sum(tile_start[:, None] >= group_end[None, :], axis=1),
                              N_EXPERTS - 1).astype(jnp.int32)
    tile_valid = (tile_start < group_end[-1]).astype(jnp.int32)

    xs = _dispatch(x, g, pos, jnp.zeros((n_rows, D_MODEL), F32), tm=tm)
    ys = _experts(xs, tile_expert, tile_valid, wg, wu, wd, tr=tr)
    return _combine(x, rt, pos, ys, tm=tm)


def _prompt_bias():
    slopes = _alibi_slopes(N_Q_HEADS)
    qi = np.arange(WINDOW)[:, None]
    kj = np.arange(2 * WINDOW)[None, :]
    dist = (qi + WINDOW - kj).astype(np.float32)
    valid = (dist >= 0) & (dist < WINDOW)
    out = np.empty((N_KV_HEADS, GQA_GROUP * WINDOW, 2 * WINDOW), np.float32)
    for g in range(N_KV_HEADS):
        for r in range(GQA_GROUP):
            b = np.where(valid, -(slopes[GQA_GROUP * g + r] * dist), -np.inf)
            out[g, r * WINDOW:(r + 1) * WINDOW] = b
    return jnp.asarray(out)


def _sample_bias(t_new, n_buf):
    slopes = _alibi_slopes(N_Q_HEADS)
    t = np.arange(t_new)[:, None]
    j = np.arange(SAMPLE_KEYS)[None, :]
    dist = (t + n_buf - j).astype(np.float32)
    valid = (dist >= 0) & (dist < WINDOW) & (j < n_buf + t_new)
    out = np.empty((N_KV_HEADS, SAMPLE_ROWS, SAMPLE_KEYS), np.float32)
    for g in range(N_KV_HEADS):
        for r in range(GQA_GROUP + 1):
            h = GQA_GROUP * g + min(r, GQA_GROUP - 1)
            out[g, r * t_new:(r + 1) * t_new] = np.where(valid, -(slopes[h] * dist), -np.inf)
    return jnp.asarray(out)


def _block_diag_pool(pool_w_l):
    bd = jnp.zeros((POOL_WIDTH, POOL_WIDTH), F32)
    for g in range(len(POOL_WINDOWS)):
        sl = slice(g * POOL_GROUP_DIM, (g + 1) * POOL_GROUP_DIM)
        bd = bd.at[sl, sl].set(pool_w_l[g])
    return bd.astype(BF16)


def kernel(x_prompt, x_sample, state_pool, state_win_k, state_win_v, norm_mix, w_in, q_norm, k_norm,
           attn_sinks, pool_w, pool_scale, w_out, norm_ffn, ffn_w_gate, ffn_w_up, ffn_w_down,
           moe_router, moe_w_gate, moe_w_up, moe_w_down):
    batch, seq, _ = x_prompt.shape
    n_seq, t_new, _ = x_sample.shape
    n_buf = state_win_k.shape[2]
    n_keep = min(WINDOW, seq)
    assert t_new == 8 and n_buf == WINDOW and SAMPLE_ROWS == (GQA_GROUP + 1) * t_new

    bias_p = _prompt_bias()
    bias_s = _sample_bias(t_new, n_buf)

    xp = x_prompt.reshape(batch * seq, D_MODEL)
    xs = x_sample.reshape(n_seq * t_new, D_MODEL)
    pool_p, k_p, v_p, pool_s, k_s, v_s = [], [], [], [], [], []
    for l in range(DEPTH):
        g_mix = norm_mix[l].reshape(1, D_MODEL)
        g_ffn = norm_ffn[l].reshape(1, D_MODEL)
        w_in_b = w_in[l].astype(BF16)
        w_out_b = w_out[l].astype(BF16)
        gain = jnp.concatenate([jnp.tile(q_norm[l], N_Q_HEADS) * (HEAD_DIM ** -0.5),
                                jnp.tile(k_norm[l], N_KV_HEADS)]).reshape(1, QK_WIDTH)
        poolw_bd = _block_diag_pool(pool_w[l])
        pscale = pool_scale[l].reshape(1, POOL_WIDTH)
        sinks = attn_sinks[l].astype(F32)
        sink_rows = jnp.repeat(
            jnp.concatenate([sinks.reshape(N_KV_HEADS, GQA_GROUP), sinks.reshape(N_KV_HEADS, GQA_GROUP)[:, -1:]], axis=1),
            t_new, axis=1)
        sinkcol = jnp.broadcast_to(sink_rows[:, :, None], (N_KV_HEADS, SAMPLE_ROWS, LANES))

        i = l // 2
        if l % 2 == 0:
            wg = ffn_w_gate[i].astype(BF16)
            wu = ffn_w_up[i].astype(BF16)
            wd = ffn_w_down[i].astype(BF16)
            channel_mix = lambda x: _ffn(x, g_ffn, wg, wu, wd, tm=MOE_ROW_TILE)
        else:
            wg = moe_w_gate[i].astype(BF16)
            wu = moe_w_up[i].astype(BF16)
            wd = moe_w_down[i].astype(BF16)
            rpad = jnp.pad(moe_router[i], ((0, 0), (0, LANES - N_EXPERTS)))
            rcat = jnp.concatenate(_split_bf16(rpad), axis=1)
            channel_mix = lambda x: _moe(x, g_ffn, rcat, wg, wu, wd)

        u, q, k, v = _project(xp, g_mix, w_in_b, gain, tm=512, q_dtype=BF16)
        xp = _prompt_mix(xp, q, k, v, u, sinks, bias_p, poolw_bd, pscale, w_out_b,
                         batch=batch, seq=seq, tq=512)
        xp = channel_mix(xp)
        pool_p.append(u.reshape(batch, seq, POOL_WIDTH)[:, seq - POOL_BUF:])
        k_p.append(k.reshape(batch, seq, N_KV_HEADS, HEAD_DIM)[:, seq - n_keep:])
        v_p.append(v.reshape(batch, seq, N_KV_HEADS, HEAD_DIM)[:, seq - n_keep:])

        u, q, k, v = _project(xs, g_mix, w_in_b, gain, tm=512, q_dtype=F32)
        xs, ko, vo, po = _sample_mix(
            xs, q, k, v, u,
            state_win_k[l].reshape(n_seq, n_buf, KV_WIDTH),
            state_win_v[l].reshape(n_seq, n_buf, KV_WIDTH),
            state_pool[l], bias_s, sinkcol, poolw_bd, pscale, w_out_b,
            n_seq=n_seq, t_new=t_new, group=16)
        xs = channel_mix(xs)
        pool_s.append(po)
        k_s.append(ko.reshape(n_seq, n_buf, N_KV_HEADS, HEAD_DIM))
        v_s.append(vo.reshape(n_seq, n_buf, N_KV_HEADS, HEAD_DIM))

    return (xp.reshape(batch, seq, D_MODEL), xs.reshape(n_seq, t_new, D_MODEL),
            jnp.stack(pool_p), jnp.stack(k_p), jnp.stack(v_p),
            jnp.stack(pool_s), jnp.stack(k_s), jnp.stack(v_s))
```
